```python
import math
import jax, jax.numpy as jnp
from jax import lax
import numpy as np

D_MODEL = 2048
BATCH = 2
SEQ = 16384
DEPTH = 1
DEC_BATCH = 16
DEC_SEQ = 16
PAST_LEN = 4096

CHUNK = 64
N_META = 16
Q_BLOCK = 128
H_A = 8
DK_A = 128
DV_A = 2 * DK_A
ROT_DIM = DK_A // 4
ROPE_THETA = 500000.0
H_R = 16
DK_R = D_MODEL // H_R
DV_R = D_MODEL // H_R
N_EXPERTS = 32
TOP_K = 4
D_FF = D_MODEL
SWIGLU_ALPHA = 1.702
SWIGLU_LIMIT = 7.0
EXPERT_BLOCK = 128
LN_EPS = 1e-5
DEEPNORM_ALPHA = (2.0 * DEPTH) ** 0.25
DEEPNORM_BETA = (8.0 * DEPTH) ** -0.25

N_QA = H_A * 2 * DK_A
N_KA = H_A * 2 * DK_A
N_VA = H_A * DV_A
N_QR = H_R * DK_R
N_FR = H_R * DK_R
N_IR = H_R * DV_R
N_GR = H_R * DV_R
OFF_KA = N_QA
OFF_VA = OFF_KA + N_KA
OFF_QR = OFF_VA + N_VA
OFF_FR = OFF_QR + N_QR
OFF_IR = OFF_FR + N_FR
OFF_GR = OFF_IR + N_IR
OFF_GA = OFF_GR + N_GR
OFF_GB = OFF_GA + D_MODEL
N_IN = OFF_GB + D_MODEL
SPLITS = (OFF_KA, OFF_VA, OFF_QR, OFF_FR, OFF_IR, OFF_GR, OFF_GA, OFF_GB)

kernel_name = "diffattn_hgrn2_moe_stream_step"

F32 = jnp.float32


def layer_norm(x, g, b):
    xf = x.astype(F32)
    mu = jnp.mean(xf, -1, keepdims=True)
    var = jnp.mean(jnp.square(xf - mu), -1, keepdims=True)
    return ((xf - mu) * lax.rsqrt(var + LN_EPS) * g + b).astype(x.dtype)


def rms_norm(x, g):
    xf = x.astype(F32)
    return xf * lax.rsqrt(jnp.mean(xf * xf, -1, keepdims=True) + LN_EPS) * g


def rope(x, pos):
    inv_freq = ROPE_THETA ** (-jnp.arange(0, ROT_DIM, 2, dtype=F32) / ROT_DIM)
    ang = pos.astype(F32)[:, None] * inv_freq[None, :]
    cos = jnp.cos(ang)[:, None, None, :]
    sin = jnp.sin(ang)[:, None, None, :]
    xf = x.astype(F32)
    x1 = xf[..., : ROT_DIM // 2]
    x2 = xf[..., ROT_DIM // 2: ROT_DIM]
    out = jnp.concatenate([x1 * cos - x2 * sin, x2 * cos + x1 * sin, xf[..., ROT_DIM:]], -1)
    return out.astype(x.dtype)


def in_proj(h, w_in, pos):
    B, L = h.shape[0], h.shape[1]
    z = jnp.einsum('bld,dn->bln', h, w_in)
    qa, ka, va, qr, fr, ir, gr, ga, gb = jnp.split(z, SPLITS, axis=-1)
    qa = rope(qa.reshape(B, L, H_A, 2, DK_A), pos)
    ka = rope(ka.reshape(B, L, H_A, 2, DK_A), pos)
    va = va.reshape(B, L, H_A, DV_A)
    qr = qr.reshape(B, L, H_R, DK_R)
    fr = fr.reshape(B, L, H_R, DK_R)
    ir = ir.reshape(B, L, H_R, DV_R)
    gr = gr.reshape(B, L, H_R, DV_R)
    return qa, ka, va, qr, fr, ir, gr, ga, gb


def meta_kv(h_meta, w_in):
    z = jnp.einsum('ld,dn->ln', h_meta, w_in[:, OFF_KA:OFF_QR])
    ka, va = jnp.split(z, [N_KA], axis=-1)
    ka = rope(ka.reshape(N_META, H_A, 2, DK_A), jnp.arange(N_META, dtype=jnp.int32))
    return ka, va.reshape(N_META, H_A, DV_A)


def diff_lambda(lam_p, lam_init):
    lp = lam_p.astype(F32)
    return jnp.exp(jnp.sum(lp[0] * lp[1])) - jnp.exp(jnp.sum(lp[2] * lp[3])) + lam_init


def diff_attn(q, k, v, lam, mask):
    s = jnp.einsum('bqhcd,bkhcd->bhcqk', q, k, preferred_element_type=F32) * (DK_A ** -0.5)
    if mask is not None:
        s = jnp.where(mask, s, -jnp.inf)
    p = jax.nn.softmax(s, axis=-1)
    a = p[:, :, 0] - lam * p[:, :, 1]
    return jnp.einsum('bhqk,bkhd->bqhd', a, v.astype(F32))


def attn_prompt(qa, ka, va, lam):
    B, N = qa.shape[0], qa.shape[1]
    L = N - N_META
    o_meta = diff_attn(qa[:, :N_META], ka[:, :N_META], va[:, :N_META], lam, None)
    if L == 0:
        return o_meta
    key_chunk = jnp.concatenate([jnp.full((N_META,), -1, jnp.int32),
                                 jnp.arange(L, dtype=jnp.int32) // CHUNK])
    nb = L // Q_BLOCK
    qb = qa[:, N_META:].reshape(B, nb, Q_BLOCK, H_A, 2, DK_A).swapaxes(0, 1)

    def one_block(args):
        q_blk, j = args
        q_chunk = (j * Q_BLOCK + jnp.arange(Q_BLOCK, dtype=jnp.int32)) // CHUNK
        return diff_attn(q_blk, ka, va, lam, key_chunk[None, :] <= q_chunk[:, None])

    o_f = lax.map(one_block, (qb, jnp.arange(nb, dtype=jnp.int32)))
    o_f = o_f.swapaxes(0, 1).reshape(B, L, H_A, DV_A)
    return jnp.concatenate([o_meta, o_f], axis=1)


def attn_sample(qa, ka, va, k_meta, v_meta, k_cache, v_cache, lam):
    s = jnp.concatenate([
        jnp.einsum('bqhcd,khcd->bhcqk', qa, k_meta, preferred_element_type=F32),
        jnp.einsum('bqhcd,bkhcd->bhcqk', qa, k_cache, preferred_element_type=F32),
        jnp.einsum('bqhcd,bkhcd->bhcqk', qa, ka, preferred_element_type=F32)], axis=-1) * (DK_A ** -0.5)
    p = jax.nn.softmax(s, axis=-1)
    a = p[:, :, 0] - lam * p[:, :, 1]
    a_m, a_c, a_n = jnp.split(a, [N_META, N_META + k_cache.shape[1]], axis=-1)
    return (jnp.einsum('bhqk,khd->bqhd', a_m, v_meta.astype(F32))
            + jnp.einsum('bhqk,bkhd->bqhd', a_c, v_cache.astype(F32))
            + jnp.einsum('bhqk,bkhd->bqhd', a_n, va.astype(F32)))


def hgrn_gates(fr, lb):
    lbh = lb.reshape(H_R, DK_R)
    f = lbh + (1.0 - lbh) * jax.nn.sigmoid(fr.astype(F32))
    return 1.0 - f, jnp.log(f)


def gla_chunk(S, q, k, v, lf):
    C = q.shape[1]
    b = jnp.cumsum(lf, axis=1)
    causal = jnp.tril(jnp.ones((C, C), bool))[None, :, :, None, None]
    decay = jnp.exp(jnp.where(causal, b[:, :, None] - b[:, None, :], -jnp.inf))
    scores = jnp.einsum('bthd,bshd,btshd->bhts', q, k, decay)
    o = (jnp.einsum('bhts,bshv->bthv', scores, v)
         + jnp.einsum('bthd,bhdv->bthv', q * jnp.exp(b), S))
    b_last = b[:, -1]
    S_new = (jnp.exp(b_last)[..., None] * S
             + jnp.einsum('bshd,bshv->bhdv', k * jnp.exp(b_last[:, None] - b), v))
    return S_new, o


def hgrn_prompt(q, k, v, lf):
    B, N = q.shape[0], q.shape[1]
    L = N - N_META
    S0 = jnp.zeros((B, H_R, DK_R, DV_R), F32)
    S, o_meta = gla_chunk(S0, q[:, :N_META], k[:, :N_META], v[:, :N_META], lf[:, :N_META])
    if L == 0:
        return S, o_meta
    nc = L // CHUNK

    def to_chunks(t):
        return t[:, N_META:].reshape(B, nc, CHUNK, t.shape[2], t.shape[3]).swapaxes(0, 1)

    def step(S_c, xs):
        q_c, k_c, v_c, lf_c = xs
        return gla_chunk(S_c, q_c, k_c, v_c, lf_c)

    S, o_f = lax.scan(step, S, (to_chunks(q), to_chunks(k), to_chunks(v), to_chunks(lf)))
    o_f = o_f.swapaxes(0, 1).reshape(B, L, H_R, DV_R)
    return S, jnp.concatenate([o_meta, o_f], axis=1)


def merge(o_a, o_r, gr, ga, gb, lp, lam_init, dt):
    B, L = o_a.shape[0], o_a.shape[1]
    ya = (rms_norm(o_a, lp['subln_g']) * (1.0 - lam_init)).astype(dt).reshape(B, L, H_A * DV_A) @ lp['w_pa']
    yr = (rms_norm(o_r, lp['rnorm_g']) * jax.nn.silu(gr.astype(F32))).astype(dt).reshape(B, L, H_R * DV_R) @ lp['w_pb']
    m = jax.nn.sigmoid(ga.astype(F32)) * ya + jax.nn.sigmoid(gb.astype(F32)) * yr
    return m.astype(dt) @ lp['w_o']


def moe(x, w_r, b_r, w1, b1, w2, b2):
    T, D = x.shape
    logits = jnp.einsum('td,de->te', x, w_r, preferred_element_type=F32) + b_r
    top_v, top_e = lax.top_k(logits, TOP_K)
    gates = jax.nn.softmax(top_v, axis=-1)
    flat_e = top_e.reshape(-1)
    order = jnp.argsort(flat_e)
    sorted_e = flat_e[order]
    counts = jnp.bincount(flat_e, length=N_EXPERTS)
    padded = (counts + EXPERT_BLOCK - 1) // EXPERT_BLOCK * EXPERT_BLOCK
    pad_end = jnp.cumsum(padded)
    start = jnp.cumsum(counts) - counts
    dest = (pad_end - padded)[sorted_e] + jnp.arange(T * TOP_K, dtype=jnp.int32) - start[sorted_e]
    n_blocks = -(-(T * TOP_K) // EXPERT_BLOCK) + N_EXPERTS
    rows = n_blocks * EXPERT_BLOCK
    row_tok = jnp.full((rows,), T, jnp.int32).at[dest].set((order // TOP_K).astype(jnp.int32))
    xb = jnp.concatenate([x, jnp.zeros((1, D), x.dtype)], axis=0)[row_tok].reshape(n_blocks, EXPERT_BLOCK, D)
    blk_e = jnp.minimum(jnp.searchsorted(pad_end, jnp.arange(n_blocks, dtype=jnp.int32) * EXPERT_BLOCK,
                                         side='right'), N_EXPERTS - 1)

    def expert(args):
        x_blk, e = args
        hcat = x_blk @ w1[e] + b1[e]
        glu = jnp.minimum(hcat[:, ::2], SWIGLU_LIMIT)
        lin = jnp.clip(hcat[:, 1::2], -SWIGLU_LIMIT, SWIGLU_LIMIT)
        return (glu * jax.nn.sigmoid(SWIGLU_ALPHA * glu) * (lin + 1.0)) @ w2[e] + b2[e]

    yb = lax.map(expert, (xb, blk_e)).reshape(rows, D)
    dest_flat = jnp.zeros_like(dest).at[order].set(dest)
    y = yb[dest_flat].reshape(T, TOP_K, D)
    return jnp.einsum('tkd,tk->td', y.astype(F32), gates).astype(x.dtype)


def post_block(x, mix, lp):
    h = layer_norm(DEEPNORM_ALPHA * x + mix, lp['ln1_g'], lp['ln1_b'])
    B, L, D = h.shape
    f = moe(h.reshape(B * L, D), lp['w_router'], lp['b_router'], lp['w1'], lp['b1'],
            lp['w2'], lp['b2']).reshape(B, L, D)
    return layer_norm(DEEPNORM_ALPHA * h + f, lp['ln2_g'], lp['ln2_b'])


def prompt_layer(x, lp, lam_init, lb):
    N = x.shape[1]
    pos = jnp.arange(N, dtype=jnp.int32)
    qa, ka, va, qr, fr, ir, gr, ga, gb = in_proj(x, lp['w_in'], pos)
    lam = diff_lambda(lp['lam'], lam_init)
    o_a = attn_prompt(qa, ka, va, lam)
    k_r, lf = hgrn_gates(fr, lb)
    S, o_r = hgrn_prompt(qr.astype(F32), k_r, ir.astype(F32), lf)
    mix = merge(o_a, o_r, gr, ga, gb, lp, lam_init, x.dtype)
    return post_block(x, mix, lp), ka, va, S


def sample_layer(x, h_meta, k_cache, v_cache, s_cache, lp, lam_init, lb):
    past = k_cache.shape[1]
    pos = N_META + past + jnp.arange(x.shape[1], dtype=jnp.int32)
    qa, ka, va, qr, fr, ir, gr, ga, gb = in_proj(x, lp['w_in'], pos)
    k_meta, v_meta = meta_kv(h_meta, lp['w_in'])
    lam = diff_lambda(lp['lam'], lam_init)
    o_a = attn_sample(qa, ka, va, k_meta, v_meta, k_cache, v_cache, lam)
    k_r, lf = hgrn_gates(fr, lb)
    S, o_r = gla_chunk(s_cache.astype(F32), qr.astype(F32), k_r, ir.astype(F32), lf)
    mix = merge(o_a, o_r, gr, ga, gb, lp, lam_init, x.dtype)
    return post_block(x, mix, lp), ka, va, S


def setup_inputs(seed: int = 0) -> dict:
    key = jax.random.key(seed)
    ks = jax.random.split(key, 32)

    def nrm(k, shape, scale):
        return jax.random.normal(k, shape, F32) * scale

    col_scale = jnp.ones((N_IN,), F32).at[OFF_VA:OFF_QR].set(DEEPNORM_BETA).at[OFF_IR:OFF_GR].set(DEEPNORM_BETA)
    return {
        "x_prompt": nrm(ks[0], (BATCH, SEQ, D_MODEL), 1.0),
        "x_sample": nrm(ks[1], (DEC_BATCH, DEC_SEQ, D_MODEL), 1.0),
        "cache_k": nrm(ks[2], (DEPTH, DEC_BATCH, PAST_LEN, H_A, 2, DK_A), 1.0),
        "cache_v": nrm(ks[3], (DEPTH, DEC_BATCH, PAST_LEN, H_A, DV_A), 1.0),
        "state_hgrn": nrm(ks[4], (DEPTH, DEC_BATCH, H_R, DK_R, DV_R), 0.5),
        "meta_tokens": nrm(ks[5], (N_META, D_MODEL), 1.0),
        "ln0_g": 1.0 + nrm(ks[6], (D_MODEL,), 0.02),
        "ln0_b": nrm(ks[7], (D_MODEL,), 0.02),
        "w_in": nrm(ks[8], (DEPTH, D_MODEL, N_IN), D_MODEL ** -0.5) * col_scale,
        "lam_qk": nrm(ks[9], (DEPTH, 4, DK_A), 0.1),
        "subln_g": 1.0 + nrm(ks[10], (DEPTH, DV_A), 0.02),
        "lb_logits": nrm(ks[11], (DEPTH + 1, H_R * DK_R), 0.5),
        "rnorm_g": 1.0 + nrm(ks[12], (DEPTH, DV_R), 0.02),
        "w_pa": nrm(ks[13], (DEPTH, H_A * DV_A, D_MODEL), (H_A * DV_A) ** -0.5 * DEEPNORM_BETA),
        "w_pb": nrm(ks[14], (DEPTH, H_R * DV_R, D_MODEL), (H_R * DV_R) ** -0.5 * DEEPNORM_BETA),
        "w_o": nrm(ks[15], (DEPTH, D_MODEL, D_MODEL), D_MODEL ** -0.5 * DEEPNORM_BETA),
        "ln1_g": 1.0 + nrm(ks[16], (DEPTH, D_MODEL), 0.02),
        "ln1_b": nrm(ks[17], (DEPTH, D_MODEL), 0.02),
        "ln2_g": 1.0 + nrm(ks[18], (DEPTH, D_MODEL), 0.02),
        "ln2_b": nrm(ks[19], (DEPTH, D_MODEL), 0.02),
        "w_router": nrm(ks[20], (DEPTH, D_MODEL, N_EXPERTS), D_MODEL ** -0.5),
        "b_router": nrm(ks[21], (DEPTH, N_EXPERTS), 0.01),
        "w1": nrm(ks[22], (DEPTH, N_EXPERTS, D_MODEL, 2 * D_FF), D_MODEL ** -0.5),
        "b1": nrm(ks[23], (DEPTH, N_EXPERTS, 2 * D_FF), 0.01),
        "w2": nrm(ks[24], (DEPTH, N_EXPERTS, D_FF, D_MODEL), D_FF ** -0.5 * DEEPNORM_BETA),
        "b2": nrm(ks[25], (DEPTH, N_EXPERTS, D_MODEL), 0.01),
    }


def reference(x_prompt, x_sample, cache_k, cache_v, state_hgrn, meta_tokens, ln0_g, ln0_b, w_in, lam_qk,
              subln_g, lb_logits, rnorm_g, w_pa, w_pb, w_o, ln1_g, ln1_b, ln2_g, ln2_b,
              w_router, b_router, w1, b1, w2, b2):
    lb_all = jnp.cumsum(jax.nn.softmax(lb_logits.astype(F32), axis=0), axis=0)
    meta = layer_norm(meta_tokens, ln0_g, ln0_b)
    B = x_prompt.shape[0]
    hp = jnp.concatenate([jnp.broadcast_to(meta.astype(x_prompt.dtype)[None], (B, N_META, D_MODEL)),
                          layer_norm(x_prompt, ln0_g, ln0_b)], axis=1)
    hs = layer_norm(x_sample, ln0_g, ln0_b)
    hm = meta[None]
    kp, vp, sp, ks_, vs_, ss_ = [], [], [], [], [], []
    for l in range(DEPTH):
        lp = {"w_in": w_in[l], "lam": lam_qk[l], "subln_g": subln_g[l], "rnorm_g": rnorm_g[l],
              "w_pa": w_pa[l], "w_pb": w_pb[l], "w_o": w_o[l],
              "ln1_g": ln1_g[l], "ln1_b": ln1_b[l], "ln2_g": ln2_g[l], "ln2_b": ln2_b[l],
              "w_router": w_router[l], "b_router": b_router[l],
              "w1": w1[l], "b1": b1[l], "w2": w2[l], "b2": b2[l]}
        lam_init = 0.8 - 0.6 * math.exp(-0.3 * l)
        hs, k_s, v_s, s_s = sample_layer(hs, hm[0], cache_k[l], cache_v[l], state_hgrn[l], lp, lam_init, lb_all[l])
        if l + 1 < DEPTH:
            hm = prompt_layer(hm, lp, lam_init, lb_all[l])[0]
        hp, k_p, v_p, s_p = prompt_layer(hp, lp, lam_init, lb_all[l])
        kp.append(k_p)
        vp.append(v_p)
        sp.append(s_p)
        ks_.append(k_s)
        vs_.append(v_s)
        ss_.append(s_s)
    y_prompt = hp[:, N_META:]
    k_prompt = jnp.stack(kp)
    v_prompt = jnp.stack(vp)
    s_prompt = jnp.stack(sp)
    k_sample = jnp.stack(ks_)
    v_sample = jnp.stack(vs_)
    s_sample = jnp.stack(ss_)
    return (y_prompt, hs, k_prompt, v_prompt, s_prompt, k_sample, v_sample, s_sample)
```

```python
import functools
import math

import jax
import jax.numpy as jnp
from jax import lax
from jax.experimental import pallas as pl
from jax.experimental.pallas import tpu as pltpu

F32 = jnp.float32
BF16 = jnp.bfloat16

D_MODEL = 2048
CHUNK = 64
SUB = 16
N_META = 16
H_A = 8
DK_A = 128
DV_A = 2 * DK_A
HEAD_A = 2 * DK_A
ROT_DIM = DK_A // 4
ROPE_THETA = 500000.0
H_R = 16
DK_R = D_MODEL // H_R
DV_R = D_MODEL // H_R
N_EXPERTS = 32
TOP_K = 4
D_FF = D_MODEL
SWIGLU_ALPHA = 1.702
SWIGLU_LIMIT = 7.0
LN_EPS = 1e-5
DEPTH = 1
DEEPNORM_ALPHA = (2.0 * DEPTH) ** 0.25
LAM_INIT = 0.8 - 0.6 * math.exp(-0.3 * 0)

SEG = D_MODEL
SEG_QA, SEG_KA, SEG_VA, SEG_QR, SEG_FR, SEG_IR, SEG_GR, SEG_GA, SEG_GB = range(9)

LANES = 128
VMEM_LIMIT_BYTES = 56 * 1024 * 1024

NT_DIMS = (((1,), (1,)), ((), ()))
TN_DIMS = (((0,), (0,)), ((), ()))


def _params(*sem):
    return pltpu.CompilerParams(dimension_semantics=sem, vmem_limit_bytes=VMEM_LIMIT_BYTES)


def _tile(n, pref, mult):
    if n <= pref:
        return n
    t = pref - pref % mult
    while t > mult and n % t:
        t -= mult
    assert n % t == 0, (n, pref, mult)
    return t


def _layer_norm(x, g, b):
    mu = jnp.mean(x, axis=-1, keepdims=True)
    xc = x - mu
    var = jnp.mean(xc * xc, axis=-1, keepdims=True)
    return xc * lax.rsqrt(var + LN_EPS) * g + b


def _ln_kernel(x_ref, g_ref, b_ref, o_ref):
    o_ref[...] = _layer_norm(x_ref[...], g_ref[...], b_ref[...]).astype(o_ref.dtype)


def _ln_bf16(x, g, b):
    m = x.shape[0]
    tm = _tile(m, 512, 16)
    return pl.pallas_call(
        _ln_kernel,
        grid=(m // tm,),
        in_specs=[pl.BlockSpec((tm, D_MODEL), lambda i: (i, 0)),
                  pl.BlockSpec((1, D_MODEL), lambda i: (0, 0)),
                  pl.BlockSpec((1, D_MODEL), lambda i: (0, 0))],
        out_specs=pl.BlockSpec((tm, D_MODEL), lambda i: (i, 0)),
        out_shape=jax.ShapeDtypeStruct((m, D_MODEL), BF16),
        compiler_params=_params("parallel"),
        name="ln0",
    )(x, g, b)


def _rope_tables(pos):
    inv_freq = ROPE_THETA ** (-jnp.arange(0, ROT_DIM, 2, dtype=F32) / ROT_DIM)
    ang = pos.astype(F32)[:, None] * inv_freq[None, :]
    n = pos.shape[0]
    cos, sin = jnp.cos(ang), jnp.sin(ang)
    rest = LANES - ROT_DIM
    cos_t = jnp.concatenate([cos, cos, jnp.ones((n, rest), F32)], axis=1)
    sin_t = jnp.concatenate([-sin, sin, jnp.zeros((n, rest), F32)], axis=1)
    return cos_t, sin_t


def _proj_kernel(*refs, mode, tn):
    x_ref, w_ref = refs[:2]
    z = jnp.dot(x_ref[...], w_ref[...], preferred_element_type=F32)
    if mode == "plain":
        (o_ref,) = refs[2:]
        o_ref[...] = z.astype(o_ref.dtype)
    elif mode == "dual":
        o32_ref, o16_ref = refs[2:]
        o32_ref[...] = z
        o16_ref[...] = z.astype(BF16)
    elif mode == "gates":
        lb_ref, k_ref, lf_ref = refs[2:]
        lb = lb_ref[...]
        f = lb + (1.0 - lb) * jax.nn.sigmoid(z)
        k_ref[...] = (1.0 - f).astype(BF16)
        lf_ref[...] = jnp.log(f)
    else:
        cos_ref, sin_ref = refs[2:4]
        outs = refs[4:]
        cos = cos_ref[...]
        sin = sin_ref[...]
        lane = lax.broadcasted_iota(jnp.int32, cos.shape, 1)
        half = ROT_DIM // 2
        for c in range(tn // LANES):
            sl = slice(c * LANES, (c + 1) * LANES)
            zc = z[:, sl]
            partner = jnp.where(lane < half, pltpu.roll(zc, LANES - half, 1), pltpu.roll(zc, half, 1))
            r = zc * cos + partner * sin
            if mode == "rope_q":
                outs[0][:, sl] = (r * (DK_A ** -0.5)).astype(BF16)
            else:
                outs[0][:, sl] = r
                outs[1][:, sl] = r.astype(BF16)


def _proj(hn, w, seg, nseg, mode, out_dtypes, period, extra=()):
    m = hn.shape[0]
    tm = _tile(period, 1024, 16)
    tn = 1024
    nj = nseg * SEG // tn
    off = seg * SEG // tn
    in_specs = [pl.BlockSpec((tm, D_MODEL), lambda i, j: (i, 0)),
                pl.BlockSpec((D_MODEL, tn), lambda i, j: (0, j + off))]
    if mode == "gates":
        in_specs.append(pl.BlockSpec((1, tn), lambda i, j: (0, j)))
    elif mode in ("rope_q", "rope_k"):
        nper = period // tm
        in_specs += [pl.BlockSpec((tm, LANES), lambda i, j: (i % nper, 0))] * 2
    out_specs = [pl.BlockSpec((tm, tn), lambda i, j: (i, j)) for _ in out_dtypes]
    out_shape = [jax.ShapeDtypeStruct((m, nseg * SEG), dt) for dt in out_dtypes]
    res = pl.pallas_call(
        functools.partial(_proj_kernel, mode=mode, tn=tn),
        grid=(m // tm, nj),
        in_specs=in_specs,
        out_specs=out_specs,
        out_shape=out_shape,
        compiler_params=_params("parallel", "parallel"),
        name="in_proj_" + mode,
    )(hn, w, *extra)
    return res


def _in_proj(hn, w_bf, lb, cos_t, sin_t, period):
    (q16,) = _proj(hn, w_bf, SEG_QA, 1, "rope_q", [BF16], period, (cos_t, sin_t))
    k32, k16 = _proj(hn, w_bf, SEG_KA, 1, "rope_k", [F32, BF16], period, (cos_t, sin_t))
    v32, v16 = _proj(hn, w_bf, SEG_VA, 1, "dual", [F32, BF16], period)
    (qr16,) = _proj(hn, w_bf, SEG_QR, 1, "plain", [BF16], period)
    kr16, lf32 = _proj(hn, w_bf, SEG_FR, 1, "gates", [BF16, F32], period, (lb,))
    (ir16,) = _proj(hn, w_bf, SEG_IR, 1, "plain", [BF16], period)
    (g32,) = _proj(hn, w_bf, SEG_GR, 3, "plain", [F32], period)
    return q16, k32, k16, v32, v16, qr16, kr16, lf32, ir16, g32


def _attn_kernel(qi_ref, ki_ref, q_ref, k_ref, v_ref, km_ref, vm_ref, lam_ref, g_ref, o_ref,
                 m_ref, l_ref, acc_ref, *, tq, tk, causal, kv_valid, n_kv):
    step = pl.program_id(2)
    qi = qi_ref[step]
    ki = ki_ref[step]
    q = q_ref[0]

    @pl.when(ki == 0)
    def _meta():
        col = lax.broadcasted_iota(jnp.int32, (tq, LANES), 1)
        ps = []
        for c in range(2):
            sl = slice(c * DK_A, (c + 1) * DK_A)
            s = lax.dot_general(q[:, sl], km_ref[:, sl], NT_DIMS, preferred_element_type=F32)
            s = jnp.where(col < N_META, s, -jnp.inf)
            m = jnp.max(s, axis=1, keepdims=True)
            p = jnp.exp(s - m)
            m_ref[c] = m
            l_ref[c] = jnp.sum(p, axis=1, keepdims=True)
            ps.append(p.astype(BF16))
        acc_ref[...] = jnp.dot(jnp.concatenate(ps, axis=0), vm_ref[...], preferred_element_type=F32)

    def block(mask):
        k = k_ref[0]
        ps = []
        for c in range(2):
            sl = slice(c * DK_A, (c + 1) * DK_A)
            s = lax.dot_general(q[:, sl], k[:, sl], NT_DIMS, preferred_element_type=F32)
            if mask is not None:
                s = jnp.where(mask, s, -jnp.inf)
            m_old = m_ref[c]
            m_new = jnp.maximum(m_old, jnp.max(s, axis=1, keepdims=True))
            alpha = jnp.exp(m_old - m_new)
            p = jnp.exp(s - m_new)
            l_ref[c] = alpha * l_ref[c] + jnp.sum(p, axis=1, keepdims=True)
            m_ref[c] = m_new
            rows = slice(c * tq, (c + 1) * tq)
            acc_ref[rows, :] = alpha * acc_ref[rows, :]
            ps.append(p.astype(BF16))
        acc_ref[...] += jnp.dot(jnp.concatenate(ps, axis=0), v_ref[0], preferred_element_type=F32)

    if causal:
        @pl.when(ki != qi)
        def _full():
            block(None)

        @pl.when(ki == qi)
        def _diag():
            row = lax.broadcasted_iota(jnp.int32, (tq, tk), 0)
            col = lax.broadcasted_iota(jnp.int32, (tq, tk), 1)
            block(col // CHUNK <= row // CHUNK)
        last = ki == qi
    else:
        col = lax.broadcasted_iota(jnp.int32, (tq, tk), 1) + ki * tk
        block(col < kv_valid)
        last = ki == n_kv - 1

    @pl.when(last)
    def _finish():
        lp = lam_ref[...]
        lam = (jnp.exp(jnp.sum(lp[0:1] * lp[1:2], axis=1, keepdims=True))
               - jnp.exp(jnp.sum(lp[2:3] * lp[3:4], axis=1, keepdims=True)) + LAM_INIT)
        o = acc_ref[0:tq, :] / l_ref[0] - lam * (acc_ref[tq:2 * tq, :] / l_ref[1])
        ms = jnp.mean(o * o, axis=1, keepdims=True)
        o_ref[0] = (o * lax.rsqrt(ms + LN_EPS) * g_ref[...] * (1.0 - LAM_INIT)).astype(o_ref.dtype)


def _diff_attn(q, k, v, k_meta, v_meta, lam_qk, subln_g, *, causal, kv_valid, tq, tk):
    bsz, lq, _ = q.shape
    lk = k.shape[1]
    nq, nk = lq // tq, lk // tk
    if causal:
        pairs = [(i, j) for i in range(nq) for j in range(i + 1)]
    else:
        pairs = [(i, j) for i in range(nq) for j in range(nk)]
    qi = jnp.asarray([p[0] for p in pairs], jnp.int32)
    ki = jnp.asarray([p[1] for p in pairs], jnp.int32)
    grid_spec = pltpu.PrefetchScalarGridSpec(
        num_scalar_prefetch=2,
        grid=(bsz, H_A, len(pairs)),
        in_specs=[
            pl.BlockSpec((1, tq, HEAD_A), lambda b, h, s, qi, ki: (b, qi[s], h)),
            pl.BlockSpec((1, tk, HEAD_A), lambda b, h, s, qi, ki: (b, ki[s], h)),
            pl.BlockSpec((1, tk, HEAD_A), lambda b, h, s, qi, ki: (b, ki[s], h)),
            pl.BlockSpec((LANES, HEAD_A), lambda b, h, s, qi, ki: (0, h)),
            pl.BlockSpec((LANES, HEAD_A), lambda b, h, s, qi, ki: (0, h)),
            pl.BlockSpec((4, DK_A), lambda b, h, s, qi, ki: (0, 0)),
            pl.BlockSpec((1, DV_A), lambda b, h, s, qi, ki: (0, 0)),
        ],
        out_specs=pl.BlockSpec((1, tq, HEAD_A), lambda b, h, s, qi, ki: (b, qi[s], h)),
        scratch_shapes=[pltpu.VMEM((2, tq, 1), F32), pltpu.VMEM((2, tq, 1), F32),
                        pltpu.VMEM((2 * tq, DV_A), F32)],
    )
    return pl.pallas_call(
        functools.partial(_attn_kernel, tq=tq, tk=tk, causal=causal, kv_valid=kv_valid, n_kv=nk),
        grid_spec=grid_spec,
        out_shape=jax.ShapeDtypeStruct((bsz, lq, H_A * HEAD_A), BF16),
        compiler_params=_params("parallel", "parallel", "arbitrary"),
        name="diff_attn_causal" if causal else "diff_attn_cache",
    )(qi, ki, q, k, v, k_meta, v_meta, lam_qk, subln_g)


def _hgrn_kernel(q_ref, k_ref, lf_ref, v_ref, gr_ref, s0_ref, g_ref, o_ref, sout_ref, st_ref, *, tb):
    t = pl.program_id(2)

    @pl.when(t == 0)
    def _load_state():
        st_ref[...] = s0_ref[0, 0].T

    ones = jnp.ones((DK_R, LANES), BF16)
    row = lax.broadcasted_iota(jnp.int32, (CHUNK, DK_R), 0)
    sub = row // SUB
    tin = row % SUB
    g = g_ref[...]
    nsub = CHUNK // SUB

    def chunk(c, carry):
        sl = pl.ds(pl.multiple_of(c * CHUNK, CHUNK), CHUNK)
        qf = q_ref[0, sl, :].astype(F32)
        kf = k_ref[0, sl, :].astype(F32)
        vb = v_ref[0, sl, :]
        vf = vb.astype(F32)
        b = lf_ref[0, sl, :]
        sh = 1
        while sh < CHUNK:
            b = b + jnp.where(row >= sh, pltpu.roll(b, sh, 0), 0.0)
            sh *= 2
        b_last = b[CHUNK - 1:CHUNK, :]
        st = st_ref[...]
        qdec = (qf * jnp.exp(b)).astype(BF16)
        o = lax.dot_general(qdec, st.astype(BF16), NT_DIMS, preferred_element_type=F32)
        ends = [b[SUB * j + SUB - 1:SUB * j + SUB, :] for j in range(nsub)]
        end_of_sub = jnp.concatenate([jnp.broadcast_to(e, (SUB, DK_R)) for e in ends], axis=0)
        kk = kf * jnp.exp(end_of_sub - b)
        qs, ks = [], []
        for j in range(nsub - 1):
            e = jnp.exp(jnp.where(sub > j, b - ends[j], -jnp.inf))
            qs.append((qf * e).astype(BF16))
            ks.append(jnp.where(sub == j, kk, 0.0).astype(BF16))
        sc = lax.dot_general(jnp.concatenate(qs, axis=1), jnp.concatenate(ks, axis=1), NT_DIMS,
                             preferred_element_type=F32)
        o = o + jnp.dot(sc.astype(BF16), vb, preferred_element_type=F32)
        k3 = kf.reshape(nsub, SUB, DK_R)
        b3 = b.reshape(nsub, SUB, DK_R)
        v3 = vf.reshape(nsub, SUB, DV_R)
        for s in range(SUB):
            shp = (nsub, SUB, DK_R)
            ks_b = jnp.broadcast_to(k3[:, s:s + 1, :], shp).reshape(CHUNK, DK_R)
            bs_b = jnp.broadcast_to(b3[:, s:s + 1, :], shp).reshape(CHUNK, DK_R)
            vs_b = jnp.broadcast_to(v3[:, s:s + 1, :], shp).reshape(CHUNK, DV_R)
            a = qf * ks_b * jnp.exp(jnp.where(tin >= s, b - bs_b, -jnp.inf))
            o = o + jnp.dot(a.astype(BF16), ones, preferred_element_type=F32) * vs_b
        kdec = (kf * jnp.exp(b_last - b)).astype(BF16)
        st_ref[...] = jnp.exp(b_last) * st + lax.dot_general(vb, kdec, TN_DIMS, preferred_element_type=F32)
        ms = jnp.mean(o * o, axis=1, keepdims=True)
        gr = gr_ref[0, sl, :]
        o_ref[0, sl, :] = (o * lax.rsqrt(ms + LN_EPS) * g * (gr * jax.nn.sigmoid(gr))).astype(o_ref.dtype)
        return carry

    lax.fori_loop(0, tb // CHUNK, chunk, 0)

    @pl.when(t == pl.num_programs(2) - 1)
    def _store_state():
        sout_ref[0, 0] = st_ref[...].T


def _hgrn(q, k, lf, v, g3, s0, rnorm_g):
    bsz, length, _ = q.shape
    tb = _tile(length, 512, CHUNK)
    tok = lambda b, h, t: (b, t, h)
    return pl.pallas_call(
        functools.partial(_hgrn_kernel, tb=tb),
        grid=(bsz, H_R, length // tb),
        in_specs=[pl.BlockSpec((1, tb, DK_R), tok), pl.BlockSpec((1, tb, DK_R), tok),
                  pl.BlockSpec((1, tb, DK_R), tok), pl.BlockSpec((1, tb, DV_R), tok),
                  pl.BlockSpec((1, tb, DV_R), tok),
                  pl.BlockSpec((1, 1, DK_R, DV_R), lambda b, h, t: (b, h, 0, 0)),
                  pl.BlockSpec((1, DV_R), lambda b, h, t: (0, 0))],
        out_specs=[pl.BlockSpec((1, tb, DV_R), tok),
                   pl.BlockSpec((1, 1, DK_R, DV_R), lambda b, h, t: (b, h, 0, 0))],
        out_shape=[jax.ShapeDtypeStruct((bsz, length, D_MODEL), BF16),
                   jax.ShapeDtypeStruct((bsz, H_R, DK_R, DV_R), F32)],
        scratch_shapes=[pltpu.VMEM((DV_R, DK_R), F32)],
        compiler_params=_params("parallel", "parallel", "arbitrary"),
        name="hgrn2",
    )(q, k, lf, v, g3, s0, rnorm_g)


def _merge_kernel(a_ref, r_ref, wa_ref, wb_ref, ga_ref, gb_ref, o_ref):
    ya = jnp.dot(a_ref[...], wa_ref[...], preferred_element_type=F32)
    yr = jnp.dot(r_ref[...], wb_ref[...], preferred_element_type=F32)
    o_ref[...] = (jax.nn.sigmoid(ga_ref[...]) * ya + jax.nn.sigmoid(gb_ref[...]) * yr).astype(o_ref.dtype)


def _merge(a_n, r_n, w_pa, w_pb, g3):
    m = a_n.shape[0]
    tm = _tile(m, 512, 16)
    tn = 1024
    nj = D_MODEL // tn
    return pl.pallas_call(
        _merge_kernel,
        grid=(m // tm, nj),
        in_specs=[pl.BlockSpec((tm, D_MODEL), lambda i, j: (i, 0)),
                  pl.BlockSpec((tm, D_MODEL), lambda i, j: (i, 0)),
                  pl.BlockSpec((D_MODEL, tn), lambda i, j: (0, j)),
                  pl.BlockSpec((D_MODEL, tn), lambda i, j: (0, j)),
                  pl.BlockSpec((tm, tn), lambda i, j: (i, j + nj)),
                  pl.BlockSpec((tm, tn), lambda i, j: (i, j + 2 * nj))],
        out_specs=pl.BlockSpec((tm, tn), lambda i, j: (i, j)),
        out_shape=jax.ShapeDtypeStruct((m, D_MODEL), BF16),
        compiler_params=_params("parallel", "parallel"),
        name="merge",
    )(a_n, r_n, w_pa, w_pb, g3, g3)


def _post_kernel(m_ref, wo_ref, x_ref, g0_ref, b0_ref, g1_ref, b1_ref, wr_ref, br_ref,
                 h32_ref, h16_ref, te_ref, tg_ref):
    mix = jnp.dot(m_ref[...], wo_ref[...], preferred_element_type=F32)
    x = _layer_norm(x_ref[...], g0_ref[...], b0_ref[...])
    h = _layer_norm(DEEPNORM_ALPHA * x + mix, g1_ref[...], b1_ref[...])
    h32_ref[...] = h
    h16_ref[...] = h.astype(BF16)
    logits = jnp.dot(h, wr_ref[...], preferred_element_type=F32, precision=lax.Precision.HIGHEST)
    logits = logits + br_ref[...]
    col = lax.broadcasted_iota(jnp.int32, logits.shape, 1)
    vals, idxs = [], []
    for _ in range(TOP_K):
        mx = jnp.max(logits, axis=1, keepdims=True)
        idx = jnp.min(jnp.where(logits == mx, col, LANES), axis=1, keepdims=True)
        vals.append(mx)
        idxs.append(idx)
        logits = jnp.where(col == idx, -jnp.inf, logits)
    es = [jnp.exp(v - vals[0]) for v in vals]
    den = es[0] + es[1] + es[2] + es[3]
    te = jnp.zeros(col.shape, jnp.int32)
    tg = jnp.zeros(col.shape, F32)
    for kk in range(TOP_K):
        te = jnp.where(col == kk, idxs[kk], te)
        tg = jnp.where(col == kk, es[kk] / den, tg)
    te_ref[...] = te
    tg_ref[...] = tg


def _post(mrg, w_o, x, ln0_g, ln0_b, ln1_g, ln1_b, w_r, b_r):
    m = mrg.shape[0]
    tm = _tile(m, 256, 16)
    row = lambda i: (i, 0)
    fix = lambda i: (0, 0)
    return pl.pallas_call(
        _post_kernel,
        grid=(m // tm,),
        in_specs=[pl.BlockSpec((tm, D_MODEL), row), pl.BlockSpec((D_MODEL, D_MODEL), fix),
                  pl.BlockSpec((tm, D_MODEL), row),
                  pl.BlockSpec((1, D_MODEL), fix), pl.BlockSpec((1, D_MODEL), fix),
                  pl.BlockSpec((1, D_MODEL), fix), pl.BlockSpec((1, D_MODEL), fix),
                  pl.BlockSpec((D_MODEL, LANES), fix), pl.BlockSpec((1, LANES), fix)],
        out_specs=[pl.BlockSpec((tm, D_MODEL), row), pl.BlockSpec((tm, D_MODEL), row),
                   pl.BlockSpec((tm, LANES), row), pl.BlockSpec((tm, LANES), row)],
        out_shape=[jax.ShapeDtypeStruct((m, D_MODEL), F32), jax.ShapeDtypeStruct((m, D_MODEL), BF16),
                   jax.ShapeDtypeStruct((m, LANES), jnp.int32), jax.ShapeDtypeStruct((m, LANES), F32)],
        compiler_params=_params("parallel"),
        name="post",
    )(mrg, w_o, x, ln0_g, ln0_b, ln1_g, ln1_b, w_r, b_r)


def _expert_kernel(be_ref, nb_ref, x_ref, wg_ref, wl_ref, bg_ref, bl_ref, w2_ref, b2_ref, o_ref, acc_ref):
    i = pl.program_id(0)
    f = pl.program_id(1)
    nf = pl.num_programs(1)
    used = i < nb_ref[0]

    @pl.when(used)
    def _compute():
        x = x_ref[...]
        hg = jnp.dot(x, wg_ref[0], preferred_element_type=F32) + bg_ref[0]
        hl = jnp.dot(x, wl_ref[0], preferred_element_type=F32) + bl_ref[0]
        glu = jnp.minimum(hg, SWIGLU_LIMIT)
        lin = jnp.clip(hl, -SWIGLU_LIMIT, SWIGLU_LIMIT)
        act = (glu * jax.nn.sigmoid(SWIGLU_ALPHA * glu) * (lin + 1.0)).astype(BF16)
        y = jnp.dot(act, w2_ref[0], preferred_element_type=F32)

        @pl.when(f == 0)
        def _first():
            acc_ref[...] = y + b2_ref[0]

        @pl.when(f > 0)
        def _rest():
            acc_ref[...] += y

        @pl.when(f == nf - 1)
        def _store():
            o_ref[...] = acc_ref[...].astype(o_ref.dtype)

    @pl.when(jnp.logical_and(jnp.logical_not(used), f == nf - 1))
    def _unused():
        o_ref[...] = jnp.zeros(o_ref.shape, o_ref.dtype)


def _experts(xb, blk_e, n_used, w1g, w1l, b1g, b1l, w2, b2, blk):
    rows = xb.shape[0]
    tf = 512
    nf = D_FF // tf
    grid_spec = pltpu.PrefetchScalarGridSpec(
        num_scalar_prefetch=2,
        grid=(rows // blk, nf),
        in_specs=[
            pl.BlockSpec((blk, D_MODEL), lambda i, f, be, nb: (i, 0)),
            pl.BlockSpec((1, D_MODEL, tf), lambda i, f, be, nb: (be[i], 0, f)),
            pl.BlockSpec((1, D_MODEL, tf), lambda i, f, be, nb: (be[i], 0, f)),
            pl.BlockSpec((1, 1, tf), lambda i, f, be, nb: (be[i], 0, f)),
            pl.BlockSpec((1, 1, tf), lambda i, f, be, nb: (be[i], 0, f)),
            pl.BlockSpec((1, tf, D_MODEL), lambda i, f, be, nb: (be[i], f, 0)),
            pl.BlockSpec((1, 1, D_MODEL), lambda i, f, be, nb: (be[i], 0, 0)),
        ],
        out_specs=pl.BlockSpec((blk, D_MODEL), lambda i, f, be, nb: (i, 0)),
        scratch_shapes=[pltpu.VMEM((blk, D_MODEL), F32)],
    )
    return pl.pallas_call(
        _expert_kernel,
        grid_spec=grid_spec,
        out_shape=jax.ShapeDtypeStruct((rows, D_MODEL), BF16),
        compiler_params=_params("parallel", "arbitrary"),
        name="experts",
    )(blk_e, n_used, xb, w1g, w1l, b1g, b1l, w2, b2)


def _combine_kernel(h_ref, y_ref, tg_ref, g_ref, b_ref, o_ref):
    tg = tg_ref[...]
    f = tg[:, 0:1] * y_ref[0].astype(F32)
    for kk in range(1, TOP_K):
        f = f + tg[:, kk:kk + 1] * y_ref[kk].astype(F32)
    o_ref[...] = _layer_norm(DEEPNORM_ALPHA * h_ref[...] + f, g_ref[...], b_ref[...])


def _combine(h32, yg, tg, ln2_g, ln2_b, row_off):
    m = h32.shape[0]
    tm = _tile(m, 256, 16)
    off = row_off // tm
    return pl.pallas_call(
        _combine_kernel,
        grid=(m // tm,),
        in_specs=[pl.BlockSpec((tm, D_MODEL), lambda i: (i, 0)),
                  pl.BlockSpec((TOP_K, tm, D_MODEL), lambda i: (0, i + off, 0)),
                  pl.BlockSpec((tm, LANES), lambda i: (i, 0)),
                  pl.BlockSpec((1, D_MODEL), lambda i: (0, 0)),
                  pl.BlockSpec((1, D_MODEL), lambda i: (0, 0))],
        out_specs=pl.BlockSpec((tm, D_MODEL), lambda i: (i, 0)),
        out_shape=jax.ShapeDtypeStruct((m, D_MODEL), F32),
        compiler_params=_params("parallel"),
        name="combine",
    )(h32, yg, tg, ln2_g, ln2_b)


def _route(top_e, blk):
    t = top_e.shape[0]
    n = t * TOP_K
    flat_e = top_e.reshape(-1)
    order = jnp.argsort(flat_e)
    sorted_e = flat_e[order]
    counts = jnp.bincount(flat_e, length=N_EXPERTS)
    padded = (counts + blk - 1) // blk * blk
    pad_end = jnp.cumsum(padded)
    start = jnp.cumsum(counts) - counts
    dest = ((pad_end - padded)[sorted_e] + jnp.arange(n, dtype=jnp.int32) - start[sorted_e]).astype(jnp.int32)
    n_blocks = -(-n // blk) + N_EXPERTS
    row_tok = jnp.zeros((n_blocks * blk,), jnp.int32).at[dest].set((order // TOP_K).astype(jnp.int32))
    blk_e = jnp.minimum(jnp.searchsorted(pad_end, jnp.arange(n_blocks, dtype=jnp.int32) * blk, side="right"),
                        N_EXPERTS - 1).astype(jnp.int32)
    dest_flat = jnp.zeros((n,), jnp.int32).at[order].set(dest)
    n_used = (pad_end[-1:] // blk).astype(jnp.int32)
    return row_tok, blk_e, n_used, dest_flat.reshape(t, TOP_K)


def _pad_rows(a, n):
    return jnp.pad(a, ((0, n - a.shape[0]),) + ((0, 0),) * (a.ndim - 1))


def _forward(x_prompt, x_sample, cache_k, cache_v, state_hgrn, meta_tokens, ln0_g, ln0_b, w_in, lam_qk,
             subln_g, lb_logits, rnorm_g, w_pa, w_pb, w_o, ln1_g, ln1_b, ln2_g, ln2_b,
             w_router, b_router, w1, b1, w2, b2, *, expert_block, attn_tile):
    bsz, seq, _ = x_prompt.shape
    dec_b, dec_t, _ = x_sample.shape
    past = cache_k.shape[2]
    n_s = dec_b * dec_t
    n_p = bsz * seq

    row = lambda a: a.reshape(1, -1).astype(F32)
    g0, b0, g1, b1n, g2, b2n = row(ln0_g), row(ln0_b), row(ln1_g[0]), row(ln1_b[0]), row(ln2_g[0]), row(ln2_b[0])
    lb = jnp.cumsum(jax.nn.softmax(lb_logits.astype(F32), axis=0), axis=0)[0].reshape(1, -1)
    w_in16 = w_in[0].astype(BF16)
    w_pa16, w_pb16, w_o16 = w_pa[0].astype(BF16), w_pb[0].astype(BF16), w_o[0].astype(BF16)
    w_r = jnp.pad(w_router[0].astype(F32), ((0, 0), (0, LANES - N_EXPERTS)))
    b_r = jnp.pad(b_router[0].astype(F32), (0, LANES - N_EXPERTS), constant_values=-jnp.inf).reshape(1, LANES)
    w1g, w1l = w1[0, :, :, 0::2].astype(BF16), w1[0, :, :, 1::2].astype(BF16)
    b1g = b1[0, :, 0::2].reshape(N_EXPERTS, 1, D_FF).astype(F32)
    b1l = b1[0, :, 1::2].reshape(N_EXPERTS, 1, D_FF).astype(F32)
    w2_16 = w2[0].astype(BF16)
    b2r = b2[0].reshape(N_EXPERTS, 1, D_MODEL).astype(F32)
    lam = lam_qk[0].astype(F32)
    sub_g = subln_g[0].reshape(1, DV_A).astype(F32)
    rn_g = rnorm_g[0].reshape(1, DV_R).astype(F32)

    xs = jnp.concatenate([x_sample.reshape(n_s, D_MODEL), meta_tokens], axis=0)
    n_sm = n_s + N_META
    pos_s = jnp.concatenate([jnp.tile(N_META + past + jnp.arange(dec_t, dtype=jnp.int32), dec_b),
                             jnp.arange(N_META, dtype=jnp.int32)])
    cos_s, sin_s = _rope_tables(pos_s)
    hs16 = _ln_bf16(xs, g0, b0)
    q_s, k32_s, k16_s, v32_s, v16_s, qr_s, kr_s, lf_s, ir_s, g3_s = _in_proj(hs16, w_in16, lb, cos_s, sin_s, n_sm)
    k_meta32, v_meta32 = k32_s[n_s:], v32_s[n_s:]
    k_meta16 = _pad_rows(k16_s[n_s:], LANES)
    v_meta16 = _pad_rows(v16_s[n_s:], LANES)

    lk = past + dec_t
    lk_pad = -(-lk // LANES) * LANES
    kv_pad = ((0, 0), (0, lk_pad - lk), (0, 0))
    k_cat = jnp.pad(jnp.concatenate([cache_k[0].reshape(dec_b, past, D_MODEL).astype(BF16),
                                     k16_s[:n_s].reshape(dec_b, dec_t, D_MODEL)], axis=1), kv_pad)
    v_cat = jnp.pad(jnp.concatenate([cache_v[0].reshape(dec_b, past, D_MODEL).astype(BF16),
                                     v16_s[:n_s].reshape(dec_b, dec_t, D_MODEL)], axis=1), kv_pad)
    a_s = _diff_attn(q_s[:n_s].reshape(dec_b, dec_t, D_MODEL), k_cat, v_cat, k_meta16, v_meta16, lam, sub_g,
                     causal=False, kv_valid=lk, tq=dec_t, tk=_tile(lk_pad, 2048, LANES))

    n_str = dec_b + 1
    chunk_pad = ((0, 0), (0, CHUNK - dec_t), (0, 0))
    to_chunk = lambda a: jnp.pad(a.reshape(n_str, dec_t, a.shape[-1]), chunk_pad)
    s0 = jnp.concatenate([state_hgrn[0].astype(F32), jnp.zeros((1, H_R, DK_R, DV_R), F32)], axis=0)
    r_s, s_out = _hgrn(to_chunk(qr_s), to_chunk(kr_s), to_chunk(lf_s), to_chunk(ir_s), to_chunk(g3_s), s0, rn_g)
    s_sample, s_meta = s_out[:dec_b], s_out[dec_b:]

    mrg_s = _merge(a_s.reshape(n_s, D_MODEL), r_s[:dec_b, :dec_t].reshape(n_s, D_MODEL), w_pa16, w_pb16, g3_s[:n_s])
    h32_s, h16_s, te_s, tg_s = _post(mrg_s, w_o16, x_sample.reshape(n_s, D_MODEL), g0, b0, g1, b1n, w_r, b_r)

    xp = x_prompt.reshape(n_p, D_MODEL)
    cos_p, sin_p = _rope_tables(N_META + jnp.arange(seq, dtype=jnp.int32))
    hp16 = _ln_bf16(xp, g0, b0)
    q_p, k32_p, k16_p, v32_p, v16_p, qr_p, kr_p, lf_p, ir_p, g3_p = _in_proj(hp16, w_in16, lb, cos_p, sin_p, seq)
    t3 = lambda a: a.reshape(bsz, seq, a.shape[-1])
    ta = _tile(seq, attn_tile, CHUNK)
    a_p = _diff_attn(t3(q_p), t3(k16_p), t3(v16_p), k_meta16, v_meta16, lam, sub_g,
                     causal=True, kv_valid=seq, tq=ta, tk=ta)
    r_p, s_prompt = _hgrn(t3(qr_p), t3(kr_p), t3(lf_p), t3(ir_p), t3(g3_p),
                          jnp.broadcast_to(s_meta, (bsz, H_R, DK_R, DV_R)), rn_g)
    mrg_p = _merge(a_p.reshape(n_p, D_MODEL), r_p.reshape(n_p, D_MODEL), w_pa16, w_pb16, g3_p)
    h32_p, h16_p, te_p, tg_p = _post(mrg_p, w_o16, xp, g0, b0, g1, b1n, w_r, b_r)

    h16 = jnp.concatenate([h16_p, h16_s], axis=0)
    top_e = jnp.concatenate([te_p[:, :TOP_K], te_s[:, :TOP_K]], axis=0)
    row_tok, blk_e, n_used, dest = _route(top_e, expert_block)
    xb = jnp.take(h16, row_tok, axis=0)
    yb = _experts(xb, blk_e, n_used, w1g, w1l, b1g, b1l, w2_16, b2r, expert_block)
    yg = jnp.take(yb, dest.T.reshape(-1), axis=0).reshape(TOP_K, n_p + n_s, D_MODEL)
    y_p = _combine(h32_p, yg, tg_p, g2, b2n, 0)
    y_s = _combine(h32_s, yg, tg_s, g2, b2n, n_p)

    k_prompt = jnp.concatenate([jnp.broadcast_to(k_meta32[None], (bsz, N_META, D_MODEL)), t3(k32_p)], axis=1)
    v_prompt = jnp.concatenate([jnp.broadcast_to(v_meta32[None], (bsz, N_META, D_MODEL)), t3(v32_p)], axis=1)
    return (y_p.reshape(bsz, seq, D_MODEL),
            y_s.reshape(dec_b, dec_t, D_MODEL),
            k_prompt.reshape(1, bsz, N_META + seq, H_A, 2, DK_A),
            v_prompt.reshape(1, bsz, N_META + seq, H_A, DV_A),
            s_prompt[None],
            k32_s[:n_s].reshape(1, dec_b, dec_t, H_A, 2, DK_A),
            v32_s[:n_s].reshape(1, dec_b, dec_t, H_A, DV_A),
            s_sample[None])


def kernel(x_prompt, x_sample, cache_k, cache_v, state_hgrn, meta_tokens, ln0_g, ln0_b, w_in, lam_qk, subln_g, lb_logits, rnorm_g, w_pa, w_pb, w_o, ln1_g, ln1_b, ln2_g, ln2_b, w_router, b_router, w1, b1, w2, b2):
    return _forward(x_prompt, x_sample, cache_k, cache_v, state_hgrn, meta_tokens, ln0_g, ln0_b, w_in, lam_qk,
                    subln_g, lb_logits, rnorm_g, w_pa, w_pb, w_o, ln1_g, ln1_b, ln2_g, ln2_b,
                    w_router, b_router, w1, b1, w2, b2, expert_block=512, attn_tile=1024)
```

```python
import functools
import math

import jax
import jax.numpy as jnp
from jax import lax
from jax.experimental import pallas as pl
from jax.experimental.pallas import tpu as pltpu

F32 = jnp.float32
BF16 = jnp.bfloat16

D_MODEL = 2048
CHUNK = 64
SUB = 16
N_META = 16
H_A = 8
DK_A = 128
DV_A = 2 * DK_A
HEAD_A = 2 * DK_A
ROT_DIM = DK_A // 4
ROPE_THETA = 500000.0
H_R = 16
DK_R = D_MODEL // H_R
DV_R = D_MODEL // H_R
N_EXPERTS = 32
TOP_K = 4
D_FF = D_MODEL
SWIGLU_ALPHA = 1.702
SWIGLU_LIMIT = 7.0
LN_EPS = 1e-5
DEPTH = 1
DEEPNORM_ALPHA = (2.0 * DEPTH) ** 0.25
LAM_INIT = 0.8 - 0.6 * math.exp(-0.3 * 0)

SEG = D_MODEL
SEG_QA, SEG_KA, SEG_VA, SEG_QR, SEG_FR, SEG_IR, SEG_GR, SEG_GA, SEG_GB = range(9)

LANES = 128
VMEM_LIMIT_BYTES = 56 * 1024 * 1024

NT_DIMS = (((1,), (1,)), ((), ()))
TN_DIMS = (((0,), (0,)), ((), ()))


def _params(*sem):
    return pltpu.CompilerParams(dimension_semantics=sem, vmem_limit_bytes=VMEM_LIMIT_BYTES)


def _tile(n, pref, mult):
    if n <= pref:
        return n
    t = pref - pref % mult
    while t > mult and n % t:
        t -= mult
    assert n % t == 0, (n, pref, mult)
    return t


def _layer_norm(x, g, b):
    mu = jnp.mean(x, axis=-1, keepdims=True)
    xc = x - mu
    var = jnp.mean(xc * xc, axis=-1, keepdims=True)
    return xc * lax.rsqrt(var + LN_EPS) * g + b


def _ln_kernel(x_ref, g_ref, b_ref, o_ref):
    o_ref[...] = _layer_norm(x_ref[...], g_ref[...], b_ref[...]).astype(o_ref.dtype)


def _ln_bf16(x, g, b):
    m = x.shape[0]
    tm = _tile(m, 512, 16)
    return pl.pallas_call(
        _ln_kernel,
        grid=(m // tm,),
        in_specs=[pl.BlockSpec((tm, D_MODEL), lambda i: (i, 0)),
                  pl.BlockSpec((1, D_MODEL), lambda i: (0, 0)),
                  pl.BlockSpec((1, D_MODEL), lambda i: (0, 0))],
        out_specs=pl.BlockSpec((tm, D_MODEL), lambda i: (i, 0)),
        out_shape=jax.ShapeDtypeStruct((m, D_MODEL), BF16),
        compiler_params=_params("parallel"),
        name="ln0",
    )(x, g, b)


def _rope_tables(pos):
    inv_freq = ROPE_THETA ** (-jnp.arange(0, ROT_DIM, 2, dtype=F32) / ROT_DIM)
    ang = pos.astype(F32)[:, None] * inv_freq[None, :]
    n = pos.shape[0]
    cos, sin = jnp.cos(ang), jnp.sin(ang)
    rest = LANES - ROT_DIM
    cos_t = jnp.concatenate([cos, cos, jnp.ones((n, rest), F32)], axis=1)
    sin_t = jnp.concatenate([-sin, sin, jnp.zeros((n, rest), F32)], axis=1)
    return cos_t, sin_t


def _proj_kernel(*refs, mode, tn):
    x_ref, w_ref = refs[:2]
    z = jnp.dot(x_ref[...], w_ref[...], preferred_element_type=F32)
    if mode == "plain":
        (o_ref,) = refs[2:]
        o_ref[...] = z.astype(o_ref.dtype)
    elif mode == "dual":
        o32_ref, o16_ref = refs[2:]
        o32_ref[...] = z
        o16_ref[...] = z.astype(BF16)
    elif mode == "gates":
        lb_ref, k_ref, lf_ref = refs[2:]
        lb = lb_ref[...]
        f = lb + (1.0 - lb) * jax.nn.sigmoid(z)
        k_ref[...] = (1.0 - f).astype(BF16)
        lf_ref[...] = jnp.log(f)
    else:
        cos_ref, sin_ref = refs[2:4]
        outs = refs[4:]
        cos = cos_ref[...]
        sin = sin_ref[...]
        lane = lax.broadcasted_iota(jnp.int32, cos.shape, 1)
        half = ROT_DIM // 2
        for c in range(tn // LANES):
            sl = slice(c * LANES, (c + 1) * LANES)
            zc = z[:, sl]
            partner = jnp.where(lane < half, pltpu.roll(zc, LANES - half, 1), pltpu.roll(zc, half, 1))
            r = zc * cos + partner * sin
            if mode == "rope_q":
                outs[0][:, sl] = (r * (DK_A ** -0.5)).astype(BF16)
            else:
                outs[0][:, sl] = r
                outs[1][:, sl] = r.astype(BF16)


def _proj(hn, w, seg, nseg, mode, out_dtypes, period, extra=()):
    m = hn.shape[0]
    tm = _tile(period, 1024, 16)
    tn = 1024
    nj = nseg * SEG // tn
    off = seg * SEG // tn
    in_specs = [pl.BlockSpec((tm, D_MODEL), lambda i, j: (i, 0)),
                pl.BlockSpec((D_MODEL, tn), lambda i, j: (0, j + off))]
    if mode == "gates":
        in_specs.append(pl.BlockSpec((1, tn), lambda i, j: (0, j)))
    elif mode in ("rope_q", "rope_k"):
        nper = period // tm
        in_specs += [pl.BlockSpec((tm, LANES), lambda i, j: (i % nper, 0))] * 2
    out_specs = [pl.BlockSpec((tm, tn), lambda i, j: (i, j)) for _ in out_dtypes]
    out_shape = [jax.ShapeDtypeStruct((m, nseg * SEG), dt) for dt in out_dtypes]
    res = pl.pallas_call(
        functools.partial(_proj_kernel, mode=mode, tn=tn),
        grid=(m // tm, nj),
        in_specs=in_specs,
        out_specs=out_specs,
        out_shape=out_shape,
        compiler_params=_params("parallel", "parallel"),
        name="in_proj_" + mode,
    )(hn, w, *extra)
    return res


def _in_proj(hn, w_bf, lb, cos_t, sin_t, period):
    (q16,) = _proj(hn, w_bf, SEG_QA, 1, "rope_q", [BF16], period, (cos_t, sin_t))
    k32, k16 = _proj(hn, w_bf, SEG_KA, 1, "rope_k", [F32, BF16], period, (cos_t, sin_t))
    v32, v16 = _proj(hn, w_bf, SEG_VA, 1, "dual", [F32, BF16], period)
    (qr16,) = _proj(hn, w_bf, SEG_QR, 1, "plain", [BF16], period)
    kr16, lf32 = _proj(hn, w_bf, SEG_FR, 1, "gates", [BF16, F32], period, (lb,))
    (ir16,) = _proj(hn, w_bf, SEG_IR, 1, "plain", [BF16], period)
    (g32,) = _proj(hn, w_bf, SEG_GR, 3, "plain", [F32], period)
    return q16, k32, k16, v32, v16, qr16, kr16, lf32, ir16, g32


def _attn_kernel(qi_ref, ki_ref, q_ref, k_ref, v_ref, km_ref, vm_ref, lam_ref, g_ref, o_ref,
                 m_ref, l_ref, acc_ref, *, tq, tk, causal, kv_valid, n_kv):
    step = pl.program_id(2)
    qi = qi_ref[step]
    ki = ki_ref[step]
    q = q_ref[0]

    @pl.when(ki == 0)
    def _meta():
        col = lax.broadcasted_iota(jnp.int32, (tq, LANES), 1)
        ps = []
        for c in range(2):
            sl = slice(c * DK_A, (c + 1) * DK_A)
            s = lax.dot_general(q[:, sl], km_ref[:, sl], NT_DIMS, preferred_element_type=F32)
            s = jnp.where(col < N_META, s, -jnp.inf)
            m = jnp.max(s, axis=1, keepdims=True)
            p = jnp.exp(s - m)
            m_ref[c] = m
            l_ref[c] = jnp.sum(p, axis=1, keepdims=True)
            ps.append(p.astype(BF16))
        acc_ref[...] = jnp.dot(jnp.concatenate(ps, axis=0), vm_ref[...], preferred_element_type=F32)

    def block(mask):
        k = k_ref[0]
        ps = []
        for c in range(2):
            sl = slice(c * DK_A, (c + 1) * DK_A)
            s = lax.dot_general(q[:, sl], k[:, sl], NT_DIMS, preferred_element_type=F32)
            if mask is not None:
                s = jnp.where(mask, s, -jnp.inf)
            m_old = m_ref[c]
            m_new = jnp.maximum(m_old, jnp.max(s, axis=1, keepdims=True))
            alpha = jnp.exp(m_old - m_new)
            p = jnp.exp(s - m_new)
            l_ref[c] = alpha * l_ref[c] + jnp.sum(p, axis=1, keepdims=True)
            m_ref[c] = m_new
            rows = slice(c * tq, (c + 1) * tq)
            acc_ref[rows, :] = alpha * acc_ref[rows, :]
            ps.append(p.astype(BF16))
        acc_ref[...] += jnp.dot(jnp.concatenate(ps, axis=0), v_ref[0], preferred_element_type=F32)

    if causal:
        @pl.when(ki != qi)
        def _full():
            block(None)

        @pl.when(ki == qi)
        def _diag():
            row = lax.broadcasted_iota(jnp.int32, (tq, tk), 0)
            col = lax.broadcasted_iota(jnp.int32, (tq, tk), 1)
            block(col // CHUNK <= row // CHUNK)
        last = ki == qi
    else:
        col = lax.broadcasted_iota(jnp.int32, (tq, tk), 1) + ki * tk
        block(col < kv_valid)
        last = ki == n_kv - 1

    @pl.when(last)
    def _finish():
        lp = lam_ref[...]
        lam = (jnp.exp(jnp.sum(lp[0:1] * lp[1:2], axis=1, keepdims=True))
               - jnp.exp(jnp.sum(lp[2:3] * lp[3:4], axis=1, keepdims=True)) + LAM_INIT)
        o = acc_ref[0:tq, :] / l_ref[0] - lam * (acc_ref[tq:2 * tq, :] / l_ref[1])
        ms = jnp.mean(o * o, axis=1, keepdims=True)
        o_ref[0] = (o * lax.rsqrt(ms + LN_EPS) * g_ref[...] * (1.0 - LAM_INIT)).astype(o_ref.dtype)


def _diff_attn(q, k, v, k_meta, v_meta, lam_qk, subln_g, *, causal, kv_valid, tq, tk):
    bsz, lq, _ = q.shape
    lk = k.shape[1]
    nq, nk = lq // tq, lk // tk
    if causal:
        pairs = [(i, j) for i in range(nq) for j in range(i + 1)]
    else:
        pairs = [(i, j) for i in range(nq) for j in range(nk)]
    qi = jnp.asarray([p[0] for p in pairs], jnp.int32)
    ki = jnp.asarray([p[1] for p in pairs], jnp.int32)
    grid_spec = pltpu.PrefetchScalarGridSpec(
        num_scalar_prefetch=2,
        grid=(bsz, H_A, len(pairs)),
        in_specs=[
            pl.BlockSpec((1, tq, HEAD_A), lambda b, h, s, qi, ki: (b, qi[s], h)),
            pl.BlockSpec((1, tk, HEAD_A), lambda b, h, s, qi, ki: (b, ki[s], h)),
            pl.BlockSpec((1, tk, HEAD_A), lambda b, h, s, qi, ki: (b, ki[s], h)),
            pl.BlockSpec((LANES, HEAD_A), lambda b, h, s, qi, ki: (0, h)),
            pl.BlockSpec((LANES, HEAD_A), lambda b, h, s, qi, ki: (0, h)),
            pl.BlockSpec((4, DK_A), lambda b, h, s, qi, ki: (0, 0)),
            pl.BlockSpec((1, DV_A), lambda b, h, s, qi, ki: (0, 0)),
        ],
        out_specs=pl.BlockSpec((1, tq, HEAD_A), lambda b, h, s, qi, ki: (b, qi[s], h)),
        scratch_shapes=[pltpu.VMEM((2, tq, 1), F32), pltpu.VMEM((2, tq, 1), F32),
                        pltpu.VMEM((2 * tq, DV_A), F32)],
    )
    return pl.pallas_call(
        functools.partial(_attn_kernel, tq=tq, tk=tk, causal=causal, kv_valid=kv_valid, n_kv=nk),
        grid_spec=grid_spec,
        out_shape=jax.ShapeDtypeStruct((bsz, lq, H_A * HEAD_A), BF16),
        compiler_params=_params("parallel", "parallel", "arbitrary"),
        name="diff_attn_causal" if causal else "diff_attn_cache",
    )(qi, ki, q, k, v, k_meta, v_meta, lam_qk, subln_g)


def _hgrn_kernel(q_ref, k_ref, lf_ref, v_ref, gr_ref, s0_ref, g_ref, o_ref, sout_ref, st_ref, *, tb):
    t = pl.program_id(2)

    @pl.when(t == 0)
    def _load_state():
        st_ref[...] = s0_ref[0, 0].T

    ones = jnp.ones((DK_R, LANES), BF16)
    row = lax.broadcasted_iota(jnp.int32, (CHUNK, DK_R), 0)
    sub = row // SUB
    tin = row % SUB
    g = g_ref[...]
    nsub = CHUNK // SUB

    def chunk(c, carry):
        sl = pl.ds(pl.multiple_of(c * CHUNK, CHUNK), CHUNK)
        qf = q_ref[0, sl, :].astype(F32)
        kf = k_ref[0, sl, :].astype(F32)
        vb = v_ref[0, sl, :]
        vf = vb.astype(F32)
        b = lf_ref[0, sl, :]
        sh = 1
        while sh < CHUNK:
            b = b + jnp.where(row >= sh, pltpu.roll(b, sh, 0), 0.0)
            sh *= 2
        b_last = b[CHUNK - 1:CHUNK, :]
        st = st_ref[...]
        qdec = (qf * jnp.exp(b)).astype(BF16)
        o = lax.dot_general(qdec, st.astype(BF16), NT_DIMS, preferred_element_type=F32)
        ends = [b[SUB * j + SUB - 1:SUB * j + SUB, :] for j in range(nsub)]
        end_of_sub = jnp.concatenate([jnp.broadcast_to(e, (SUB, DK_R)) for e in ends], axis=0)
        kk = kf * jnp.exp(end_of_sub - b)
        qs, ks = [], []
        for j in range(nsub - 1):
            e = jnp.exp(jnp.where(sub > j, b - ends[j], -jnp.inf))
            qs.append((qf * e).astype(BF16))
            ks.append(jnp.where(sub == j, kk, 0.0).astype(BF16))
        sc = lax.dot_general(jnp.concatenate(qs, axis=1), jnp.concatenate(ks, axis=1), NT_DIMS,
                             preferred_element_type=F32)
        o = o + jnp.dot(sc.astype(BF16), vb, preferred_element_type=F32)
        k3 = kf.reshape(nsub, SUB, DK_R)
        b3 = b.reshape(nsub, SUB, DK_R)
        v3 = vf.reshape(nsub, SUB, DV_R)
        for s in range(SUB):
            shp = (nsub, SUB, DK_R)
            ks_b = jnp.broadcast_to(k3[:, s:s + 1, :], shp).reshape(CHUNK, DK_R)
            bs_b = jnp.broadcast_to(b3[:, s:s + 1, :], shp).reshape(CHUNK, DK_R)
            vs_b = jnp.broadcast_to(v3[:, s:s + 1, :], shp).reshape(CHUNK, DV_R)
            a = qf * ks_b * jnp.exp(jnp.where(tin >= s, b - bs_b, -jnp.inf))
            o = o + jnp.dot(a.astype(BF16), ones, preferred_element_type=F32) * vs_b
        kdec = (kf * jnp.exp(b_last - b)).astype(BF16)
        st_ref[...] = jnp.exp(b_last) * st + lax.dot_general(vb, kdec, TN_DIMS, preferred_element_type=F32)
        ms = jnp.mean(o * o, axis=1, keepdims=True)
        gr = gr_ref[0, sl, :]
        o_ref[0, sl, :] = (o * lax.rsqrt(ms + LN_EPS) * g * (gr * jax.nn.sigmoid(gr))).astype(o_ref.dtype)
        return carry

    lax.fori_loop(0, tb // CHUNK, chunk, 0)

    @pl.when(t == pl.num_programs(2) - 1)
    def _store_state():
        sout_ref[0, 0] = st_ref[...].T


def _hgrn(q, k, lf, v, g3, s0, rnorm_g):
    bsz, length, _ = q.shape
    tb = _tile(length, 512, CHUNK)
    tok = lambda b, h, t: (b, t, h)
    return pl.pallas_call(
        functools.partial(_hgrn_kernel, tb=tb),
        grid=(bsz, H_R, length // tb),
        in_specs=[pl.BlockSpec((1, tb, DK_R), tok), pl.BlockSpec((1, tb, DK_R), tok),
                  pl.BlockSpec((1, tb, DK_R), tok), pl.BlockSpec((1, tb, DV_R), tok),
                  pl.BlockSpec((1, tb, DV_R), tok),
                  pl.BlockSpec((1, 1, DK_R, DV_R), lambda b, h, t: (b, h, 0, 0)),
                  pl.BlockSpec((1, DV_R), lambda b, h, t: (0, 0))],
        out_specs=[pl.BlockSpec((1, tb, DV_R), tok),
                   pl.BlockSpec((1, 1, DK_R, DV_R), lambda b, h, t: (b, h, 0, 0))],
        out_shape=[jax.ShapeDtypeStruct((bsz, length, D_MODEL), BF16),
                   jax.ShapeDtypeStruct((bsz, H_R, DK_R, DV_R), F32)],
        scratch_shapes=[pltpu.VMEM((DV_R, DK_R), F32)],
        compiler_params=_params("parallel", "parallel", "arbitrary"),
        name="hgrn2",
    )(q, k, lf, v, g3, s0, rnorm_g)


def _merge_kernel(a_ref, r_ref, wa_ref, wb_ref, ga_ref, gb_ref, o_ref):
    ya = jnp.dot(a_ref[...], wa_ref[...], preferred_element_type=F32)
    yr = jnp.dot(r_ref[...], wb_ref[...], preferred_element_type=F32)
    o_ref[...] = (jax.nn.sigmoid(ga_ref[...]) * ya + jax.nn.sigmoid(gb_ref[...]) * yr).astype(o_ref.dtype)


def _merge(a_n, r_n, w_pa, w_pb, g3):
    m = a_n.shape[0]
    tm = _tile(m, 512, 16)
    tn = 1024
    nj = D_MODEL // tn
    return pl.pallas_call(
        _merge_kernel,
        grid=(m // tm, nj),
        in_specs=[pl.BlockSpec((tm, D_MODEL), lambda i, j: (i, 0)),
                  pl.BlockSpec((tm, D_MODEL), lambda i, j: (i, 0)),
                  pl.BlockSpec((D_MODEL, tn), lambda i, j: (0, j)),
                  pl.BlockSpec((D_MODEL, tn), lambda i, j: (0, j)),
                  pl.BlockSpec((tm, tn), lambda i, j: (i, j + nj)),
                  pl.BlockSpec((tm, tn), lambda i, j: (i, j + 2 * nj))],
        out_specs=pl.BlockSpec((tm, tn), lambda i, j: (i, j)),
        out_shape=jax.ShapeDtypeStruct((m, D_MODEL), BF16),
        compiler_params=_params("parallel", "parallel"),
        name="merge",
    )(a_n, r_n, w_pa, w_pb, g3, g3)


def _post_kernel(m_ref, wo_ref, x_ref, g0_ref, b0_ref, g1_ref, b1_ref, wr_ref, br_ref,
                 h32_ref, te_ref, tg_ref):
    mix = jnp.dot(m_ref[...], wo_ref[...], preferred_element_type=F32)
    x = _layer_norm(x_ref[...], g0_ref[...], b0_ref[...])
    h = _layer_norm(DEEPNORM_ALPHA * x + mix, g1_ref[...], b1_ref[...])
    h32_ref[...] = h
    logits = jnp.dot(h, wr_ref[...], preferred_element_type=F32, precision=lax.Precision.HIGHEST)
    logits = logits + br_ref[...]
    col = lax.broadcasted_iota(jnp.int32, logits.shape, 1)
    vals, idxs = [], []
    for _ in range(TOP_K):
        mx = jnp.max(logits, axis=1, keepdims=True)
        idx = jnp.min(jnp.where(logits == mx, col, LANES), axis=1, keepdims=True)
        vals.append(mx)
        idxs.append(idx)
        logits = jnp.where(col == idx, -jnp.inf, logits)
    es = [jnp.exp(v - vals[0]) for v in vals]
    den = es[0] + es[1] + es[2] + es[3]
    te = jnp.zeros(col.shape, jnp.int32)
    tg = jnp.zeros(col.shape, F32)
    for kk in range(TOP_K):
        te = jnp.where(col == kk, idxs[kk], te)
        tg = jnp.where(col == kk, es[kk] / den, tg)
    te_ref[...] = te
    tg_ref[...] = tg


def _post(mrg, w_o, x, ln0_g, ln0_b, ln1_g, ln1_b, w_r, b_r):
    m = mrg.shape[0]
    tm = _tile(m, 256, 16)
    row = lambda i: (i, 0)
    fix = lambda i: (0, 0)
    return pl.pallas_call(
        _post_kernel,
        grid=(m // tm,),
        in_specs=[pl.BlockSpec((tm, D_MODEL), row), pl.BlockSpec((D_MODEL, D_MODEL), fix),
                  pl.BlockSpec((tm, D_MODEL), row),
                  pl.BlockSpec((1, D_MODEL), fix), pl.BlockSpec((1, D_MODEL), fix),
                  pl.BlockSpec((1, D_MODEL), fix), pl.BlockSpec((1, D_MODEL), fix),
                  pl.BlockSpec((D_MODEL, LANES), fix), pl.BlockSpec((1, LANES), fix)],
        out_specs=[pl.BlockSpec((tm, D_MODEL), row),
                   pl.BlockSpec((tm, LANES), row), pl.BlockSpec((tm, LANES), row)],
        out_shape=[jax.ShapeDtypeStruct((m, D_MODEL), F32),
                   jax.ShapeDtypeStruct((m, LANES), jnp.int32), jax.ShapeDtypeStruct((m, LANES), F32)],
        compiler_params=_params("parallel"),
        name="post",
    )(mrg, w_o, x, ln0_g, ln0_b, ln1_g, ln1_b, w_r, b_r)


def _deint_kernel(w_ref, p_ref, g_ref, l_ref, *, tc):
    y = jnp.dot(w_ref[0].astype(BF16), p_ref[...], preferred_element_type=F32)
    g_ref[0] = y[:, :tc].astype(BF16)
    l_ref[0] = y[:, tc:].astype(BF16)


def _deinterleave(w1):
    tc, tr = 512, 1024
    r = lax.broadcasted_iota(jnp.int32, (2 * tc, 2 * tc), 0)
    c = lax.broadcasted_iota(jnp.int32, (2 * tc, 2 * tc), 1)
    perm = jnp.where(c < tc, r == 2 * c, r == 2 * (c - tc) + 1).astype(BF16)
    out = jax.ShapeDtypeStruct((N_EXPERTS, D_MODEL, D_FF), BF16)
    return pl.pallas_call(
        functools.partial(_deint_kernel, tc=tc),
        grid=(N_EXPERTS, D_MODEL // tr, D_FF // tc),
        in_specs=[pl.BlockSpec((1, tr, 2 * tc), lambda e, i, j: (e, i, j)),
                  pl.BlockSpec((2 * tc, 2 * tc), lambda e, i, j: (0, 0))],
        out_specs=[pl.BlockSpec((1, tr, tc), lambda e, i, j: (e, i, j)),
                   pl.BlockSpec((1, tr, tc), lambda e, i, j: (e, i, j))],
        out_shape=[out, out],
        compiler_params=_params("parallel", "parallel", "parallel"),
        name="w1_deinterleave",
    )(w1, perm)


def _gather_kernel(idx_ref, src_ref, o_ref, sem, *, g):
    def row_copy(r, src_row):
        return pltpu.make_async_copy(src_ref.at[pl.ds(src_row, 1)], o_ref.at[pl.ds(r, 1)], sem)

    def issue(r, carry):
        row_copy(r, idx_ref[r]).start()
        return carry

    def drain(r, carry):
        row_copy(r, 0).wait()
        return carry

    lax.fori_loop(0, g, issue, 0)
    lax.fori_loop(0, g, drain, 0)


def _gather_rows(src, idx):
    n = idx.shape[0]
    g = _tile(n, 512, 128)
    return pl.pallas_call(
        functools.partial(_gather_kernel, g=g),
        grid=(n // g,),
        in_specs=[pl.BlockSpec((g,), lambda i: (i,), memory_space=pltpu.SMEM),
                  pl.BlockSpec(memory_space=pl.ANY)],
        out_specs=pl.BlockSpec((g, D_MODEL), lambda i: (i, 0)),
        out_shape=jax.ShapeDtypeStruct((n, D_MODEL), src.dtype),
        scratch_shapes=[pltpu.SemaphoreType.DMA(())],
        compiler_params=_params("arbitrary"),
        name="gather_rows",
    )(idx, src)


def _expert_kernel(be_ref, nb_ref, x_ref, wg_ref, wl_ref, bg_ref, bl_ref, w2_ref, b2_ref, o_ref, acc_ref, x16_ref):
    i = pl.program_id(0)
    f = pl.program_id(1)
    nf = pl.num_programs(1)
    used = i < nb_ref[0]

    @pl.when(jnp.logical_and(used, f == 0))
    def _cast():
        x16_ref[...] = x_ref[...].astype(BF16)

    @pl.when(used)
    def _compute():
        x = x16_ref[...]
        hg = jnp.dot(x, wg_ref[0], preferred_element_type=F32) + bg_ref[0]
        hl = jnp.dot(x, wl_ref[0], preferred_element_type=F32) + bl_ref[0]
        glu = jnp.minimum(hg, SWIGLU_LIMIT)
        lin = jnp.clip(hl, -SWIGLU_LIMIT, SWIGLU_LIMIT)
        act = (glu * jax.nn.sigmoid(SWIGLU_ALPHA * glu) * (lin + 1.0)).astype(BF16)
        y = jnp.dot(act, w2_ref[0], preferred_element_type=F32)

        @pl.when(f == 0)
        def _first():
            acc_ref[...] = y + b2_ref[0]

        @pl.when(f > 0)
        def _rest():
            acc_ref[...] += y

        @pl.when(f == nf - 1)
        def _store():
            o_ref[...] = acc_ref[...].astype(o_ref.dtype)

    @pl.when(jnp.logical_and(jnp.logical_not(used), f == nf - 1))
    def _unused():
        o_ref[...] = jnp.zeros(o_ref.shape, o_ref.dtype)


def _experts(xb, blk_e, n_used, w1g, w1l, b1g, b1l, w2, b2, blk):
    rows = xb.shape[0]
    tf = 512
    nf = D_FF // tf
    grid_spec = pltpu.PrefetchScalarGridSpec(
        num_scalar_prefetch=2,
        grid=(rows // blk, nf),
        in_specs=[
            pl.BlockSpec((blk, D_MODEL), lambda i, f, be, nb: (i, 0)),
            pl.BlockSpec((1, D_MODEL, tf), lambda i, f, be, nb: (be[i], 0, f)),
            pl.BlockSpec((1, D_MODEL, tf), lambda i, f, be, nb: (be[i], 0, f)),
            pl.BlockSpec((1, 1, tf), lambda i, f, be, nb: (be[i], 0, f)),
            pl.BlockSpec((1, 1, tf), lambda i, f, be, nb: (be[i], 0, f)),
            pl.BlockSpec((1, tf, D_MODEL), lambda i, f, be, nb: (be[i], f, 0)),
            pl.BlockSpec((1, 1, D_MODEL), lambda i, f, be, nb: (be[i], 0, 0)),
        ],
        out_specs=pl.BlockSpec((blk, D_MODEL), lambda i, f, be, nb: (i, 0)),
        scratch_shapes=[pltpu.VMEM((blk, D_MODEL), F32), pltpu.VMEM((blk, D_MODEL), BF16)],
    )
    return pl.pallas_call(
        _expert_kernel,
        grid_spec=grid_spec,
        out_shape=jax.ShapeDtypeStruct((rows, D_MODEL), F32),
        compiler_params=_params("parallel", "arbitrary"),
        name="experts",
    )(blk_e, n_used, xb, w1g, w1l, b1g, b1l, w2, b2)


def _combine_kernel(h_ref, y_ref, tg_ref, g_ref, b_ref, o_ref):
    tg = tg_ref[...]
    f = tg[:, 0:1] * y_ref[0].astype(F32)
    for kk in range(1, TOP_K):
        f = f + tg[:, kk:kk + 1] * y_ref[kk].astype(F32)
    o_ref[...] = _layer_norm(DEEPNORM_ALPHA * h_ref[...] + f, g_ref[...], b_ref[...])


def _combine(h32, yg, tg, ln2_g, ln2_b, row_off):
    m = h32.shape[0]
    tm = _tile(m, 256, 16)
    off = row_off // tm
    return pl.pallas_call(
        _combine_kernel,
        grid=(m // tm,),
        in_specs=[pl.BlockSpec((tm, D_MODEL), lambda i: (i, 0)),
                  pl.BlockSpec((TOP_K, tm, D_MODEL), lambda i: (0, i + off, 0)),
                  pl.BlockSpec((tm, LANES), lambda i: (i, 0)),
                  pl.BlockSpec((1, D_MODEL), lambda i: (0, 0)),
                  pl.BlockSpec((1, D_MODEL), lambda i: (0, 0))],
        out_specs=pl.BlockSpec((tm, D_MODEL), lambda i: (i, 0)),
        out_shape=jax.ShapeDtypeStruct((m, D_MODEL), F32),
        compiler_params=_params("parallel"),
        name="combine",
    )(h32, yg, tg, ln2_g, ln2_b)


def _route(top_e, blk):
    t = top_e.shape[0]
    n = t * TOP_K
    flat_e = top_e.reshape(-1)
    order = jnp.argsort(flat_e)
    sorted_e = flat_e[order]
    counts = jnp.bincount(flat_e, length=N_EXPERTS)
    padded = (counts + blk - 1) // blk * blk
    pad_end = jnp.cumsum(padded)
    start = jnp.cumsum(counts) - counts
    dest = ((pad_end - padded)[sorted_e] + jnp.arange(n, dtype=jnp.int32) - start[sorted_e]).astype(jnp.int32)
    n_blocks = -(-n // blk) + N_EXPERTS
    row_tok = jnp.zeros((n_blocks * blk,), jnp.int32).at[dest].set((order // TOP_K).astype(jnp.int32))
    blk_e = jnp.minimum(jnp.searchsorted(pad_end, jnp.arange(n_blocks, dtype=jnp.int32) * blk, side="right"),
                        N_EXPERTS - 1).astype(jnp.int32)
    dest_flat = jnp.zeros((n,), jnp.int32).at[order].set(dest)
    n_used = (pad_end[-1:] // blk).astype(jnp.int32)
    return row_tok, blk_e, n_used, dest_flat.reshape(t, TOP_K)


def _pad_rows(a, n):
    return jnp.pad(a, ((0, n - a.shape[0]),) + ((0, 0),) * (a.ndim - 1))


def _forward(x_prompt, x_sample, cache_k, cache_v, state_hgrn, meta_tokens, ln0_g, ln0_b, w_in, lam_qk,
             subln_g, lb_logits, rnorm_g, w_pa, w_pb, w_o, ln1_g, ln1_b, ln2_g, ln2_b,
             w_router, b_router, w1, b1, w2, b2, *, expert_block, attn_tile):
    bsz, seq, _ = x_prompt.shape
    dec_b, dec_t, _ = x_sample.shape
    past = cache_k.shape[2]
    n_s = dec_b * dec_t
    n_p = bsz * seq

    row = lambda a: a.reshape(1, -1).astype(F32)
    g0, b0, g1, b1n, g2, b2n = row(ln0_g), row(ln0_b), row(ln1_g[0]), row(ln1_b[0]), row(ln2_g[0]), row(ln2_b[0])
    lb = jnp.cumsum(jax.nn.softmax(lb_logits.astype(F32), axis=0), axis=0)[0].reshape(1, -1)
    w_in16 = w_in[0].astype(BF16)
    w_pa16, w_pb16, w_o16 = w_pa[0].astype(BF16), w_pb[0].astype(BF16), w_o[0].astype(BF16)
    w_r = jnp.pad(w_router[0].astype(F32), ((0, 0), (0, LANES - N_EXPERTS)))
    b_r = jnp.pad(b_router[0].astype(F32), (0, LANES - N_EXPERTS), constant_values=-jnp.inf).reshape(1, LANES)
    w1g, w1l = _deinterleave(w1[0])
    b1g = b1[0, :, 0::2].reshape(N_EXPERTS, 1, D_FF).astype(F32)
    b1l = b1[0, :, 1::2].reshape(N_EXPERTS, 1, D_FF).astype(F32)
    w2_16 = w2[0].astype(BF16)
    b2r = b2[0].reshape(N_EXPERTS, 1, D_MODEL).astype(F32)
    lam = lam_qk[0].astype(F32)
    sub_g = subln_g[0].reshape(1, DV_A).astype(F32)
    rn_g = rnorm_g[0].reshape(1, DV_R).astype(F32)

    xs = jnp.concatenate([x_sample.reshape(n_s, D_MODEL), meta_tokens], axis=0)
    n_sm = n_s + N_META
    pos_s = jnp.concatenate([jnp.tile(N_META + past + jnp.arange(dec_t, dtype=jnp.int32), dec_b),
                             jnp.arange(N_META, dtype=jnp.int32)])
    cos_s, sin_s = _rope_tables(pos_s)
    hs16 = _ln_bf16(xs, g0, b0)
    q_s, k32_s, k16_s, v32_s, v16_s, qr_s, kr_s, lf_s, ir_s, g3_s = _in_proj(hs16, w_in16, lb, cos_s, sin_s, n_sm)
    k_meta32, v_meta32 = k32_s[n_s:], v32_s[n_s:]
    k_meta16 = _pad_rows(k16_s[n_s:], LANES)
    v_meta16 = _pad_rows(v16_s[n_s:], LANES)

    lk = past + dec_t
    lk_pad = -(-lk // LANES) * LANES
    kv_pad = ((0, 0), (0, lk_pad - lk), (0, 0))
    k_cat = jnp.pad(jnp.concatenate([cache_k[0].reshape(dec_b, past, D_MODEL).astype(BF16),
                                     k16_s[:n_s].reshape(dec_b, dec_t, D_MODEL)], axis=1), kv_pad)
    v_cat = jnp.pad(jnp.concatenate([cache_v[0].reshape(dec_b, past, D_MODEL).astype(BF16),
                                     v16_s[:n_s].reshape(dec_b, dec_t, D_MODEL)], axis=1), kv_pad)
    a_s = _diff_attn(q_s[:n_s].reshape(dec_b, dec_t, D_MODEL), k_cat, v_cat, k_meta16, v_meta16, lam, sub_g,
                     causal=False, kv_valid=lk, tq=dec_t, tk=_tile(lk_pad, 2048, LANES))

    n_str = dec_b + 1
    chunk_pad = ((0, 0), (0, CHUNK - dec_t), (0, 0))
    to_chunk = lambda a: jnp.pad(a.reshape(n_str, dec_t, a.shape[-1]), chunk_pad)
    s0 = jnp.concatenate([state_hgrn[0].astype(F32), jnp.zeros((1, H_R, DK_R, DV_R), F32)], axis=0)
    r_s, s_out = _hgrn(to_chunk(qr_s), to_chunk(kr_s), to_chunk(lf_s), to_chunk(ir_s), to_chunk(g3_s), s0, rn_g)
    s_sample, s_meta = s_out[:dec_b], s_out[dec_b:]

    mrg_s = _merge(a_s.reshape(n_s, D_MODEL), r_s[:dec_b, :dec_t].reshape(n_s, D_MODEL), w_pa16, w_pb16, g3_s[:n_s])
    h32_s, te_s, tg_s = _post(mrg_s, w_o16, x_sample.reshape(n_s, D_MODEL), g0, b0, g1, b1n, w_r, b_r)

    xp = x_prompt.reshape(n_p, D_MODEL)
    cos_p, sin_p = _rope_tables(N_META + jnp.arange(seq, dtype=jnp.int32))
    hp16 = _ln_bf16(xp, g0, b0)
    q_p, k32_p, k16_p, v32_p, v16_p, qr_p, kr_p, lf_p, ir_p, g3_p = _in_proj(hp16, w_in16, lb, cos_p, sin_p, seq)
    t3 = lambda a: a.reshape(bsz, seq, a.shape[-1])
    ta = _tile(seq, attn_tile, CHUNK)
    a_p = _diff_attn(t3(q_p), t3(k16_p), t3(v16_p), k_meta16, v_meta16, lam, sub_g,
                     causal=True, kv_valid=seq, tq=ta, tk=ta)
    r_p, s_prompt = _hgrn(t3(qr_p), t3(kr_p), t3(lf_p), t3(ir_p), t3(g3_p),
                          jnp.broadcast_to(s_meta, (bsz, H_R, DK_R, DV_R)), rn_g)
    mrg_p = _merge(a_p.reshape(n_p, D_MODEL), r_p.reshape(n_p, D_MODEL), w_pa16, w_pb16, g3_p)
    h32_p, te_p, tg_p = _post(mrg_p, w_o16, xp, g0, b0, g1, b1n, w_r, b_r)

    h32 = jnp.concatenate([h32_p, h32_s], axis=0)
    top_e = jnp.concatenate([te_p[:, :TOP_K], te_s[:, :TOP_K]], axis=0)
    row_tok, blk_e, n_used, dest = _route(top_e, expert_block)
    xb = _gather_rows(h32, row_tok)
    yb = _experts(xb, blk_e, n_used, w1g, w1l, b1g, b1l, w2_16, b2r, expert_block)
    yg = _gather_rows(yb, dest.T.reshape(-1)).reshape(TOP_K, n_p + n_s, D_MODEL)
    y_p = _combine(h32_p, yg, tg_p, g2, b2n, 0)
    y_s = _combine(h32_s, yg, tg_s, g2, b2n, n_p)

    k_prompt = jnp.concatenate([jnp.broadcast_to(k_meta32[None], (bsz, N_META, D_MODEL)), t3(k32_p)], axis=1)
    v_prompt = jnp.concatenate([jnp.broadcast_to(v_meta32[None], (bsz, N_META, D_MODEL)), t3(v32_p)], axis=1)
    return (y_p.reshape(bsz, seq, D_MODEL),
            y_s.reshape(dec_b, dec_t, D_MODEL),
            k_prompt.reshape(1, bsz, N_META + seq, H_A, 2, DK_A),
            v_prompt.reshape(1, bsz, N_META + seq, H_A, DV_A),
            s_prompt[None],
            k32_s[:n_s].reshape(1, dec_b, dec_t, H_A, 2, DK_A),
            v32_s[:n_s].reshape(1, dec_b, dec_t, H_A, DV_A),
            s_sample[None])


def kernel(x_prompt, x_sample, cache_k, cache_v, state_hgrn, meta_tokens, ln0_g, ln0_b, w_in, lam_qk, subln_g, lb_logits, rnorm_g, w_pa, w_pb, w_o, ln1_g, ln1_b, ln2_g, ln2_b, w_router, b_router, w1, b1, w2, b2):
    return _forward(x_prompt, x_sample, cache_k, cache_v, state_hgrn, meta_tokens, ln0_g, ln0_b, w_in, lam_qk,
                    subln_g, lb_logits, rnorm_g, w_pa, w_pb, w_o, ln1_g, ln1_b, ln2_g, ln2_b,
                    w_router, b_router, w1, b1, w2, b2, expert_block=512, attn_tile=1024)
```

```python
import functools
import math

import jax
import jax.numpy as jnp
from jax import lax
from jax.experimental import pallas as pl
from jax.experimental.pallas import tpu as pltpu

F32 = jnp.float32
BF16 = jnp.bfloat16

D_MODEL = 2048
CHUNK = 64
SUB = 16
N_META = 16
H_A = 8
DK_A = 128
DV_A = 2 * DK_A
HEAD_A = 2 * DK_A
ROT_DIM = DK_A // 4
ROPE_THETA = 500000.0
H_R = 16
DK_R = D_MODEL // H_R
DV_R = D_MODEL // H_R
N_EXPERTS = 32
TOP_K = 4
D_FF = D_MODEL
SWIGLU_ALPHA = 1.702
SWIGLU_LIMIT = 7.0
LN_EPS = 1e-5
DEPTH = 1
DEEPNORM_ALPHA = (2.0 * DEPTH) ** 0.25
LAM_INIT = 0.8 - 0.6 * math.exp(-0.3 * 0)

SEG = D_MODEL
SEG_QA, SEG_KA, SEG_VA, SEG_QR, SEG_FR, SEG_IR, SEG_GR, SEG_GA, SEG_GB = range(9)

Q_SCALE = DK_A ** -0.5 * math.log2(math.e)
PV_PIECE = 256

LANES = 128
VMEM_LIMIT_BYTES = 56 * 1024 * 1024

NT_DIMS = (((1,), (1,)), ((), ()))
TN_DIMS = (((0,), (0,)), ((), ()))


def _params(*sem):
    return pltpu.CompilerParams(dimension_semantics=sem, vmem_limit_bytes=VMEM_LIMIT_BYTES)


def _tile(n, pref, mult):
    if n <= pref:
        return n
    t = pref - pref % mult
    while t > mult and n % t:
        t -= mult
    assert n % t == 0, (n, pref, mult)
    return t


def _layer_norm(x, g, b):
    mu = jnp.mean(x, axis=-1, keepdims=True)
    xc = x - mu
    var = jnp.mean(xc * xc, axis=-1, keepdims=True)
    return xc * lax.rsqrt(var + LN_EPS) * g + b


def _ln_kernel(x_ref, g_ref, b_ref, o_ref):
    o_ref[...] = _layer_norm(x_ref[...], g_ref[...], b_ref[...]).astype(o_ref.dtype)


def _ln_bf16(x, g, b):
    m = x.shape[0]
    tm = _tile(m, 512, 16)
    return pl.pallas_call(
        _ln_kernel,
        grid=(m // tm,),
        in_specs=[pl.BlockSpec((tm, D_MODEL), lambda i: (i, 0)),
                  pl.BlockSpec((1, D_MODEL), lambda i: (0, 0)),
                  pl.BlockSpec((1, D_MODEL), lambda i: (0, 0))],
        out_specs=pl.BlockSpec((tm, D_MODEL), lambda i: (i, 0)),
        out_shape=jax.ShapeDtypeStruct((m, D_MODEL), BF16),
        compiler_params=_params("parallel"),
        name="ln0",
    )(x, g, b)


def _rope_tables(pos):
    inv_freq = ROPE_THETA ** (-jnp.arange(0, ROT_DIM, 2, dtype=F32) / ROT_DIM)
    ang = pos.astype(F32)[:, None] * inv_freq[None, :]
    n = pos.shape[0]
    cos, sin = jnp.cos(ang), jnp.sin(ang)
    rest = LANES - ROT_DIM
    cos_t = jnp.concatenate([cos, cos, jnp.ones((n, rest), F32)], axis=1)
    sin_t = jnp.concatenate([-sin, sin, jnp.zeros((n, rest), F32)], axis=1)
    return cos_t, sin_t


def _proj_kernel(*refs, mode, tn):
    x_ref, w_ref = refs[:2]
    z = jnp.dot(x_ref[...], w_ref[...], preferred_element_type=F32)
    if mode == "plain":
        (o_ref,) = refs[2:]
        o_ref[...] = z.astype(o_ref.dtype)
    elif mode == "dual":
        o32_ref, o16_ref = refs[2:]
        o32_ref[...] = z
        o16_ref[...] = z.astype(BF16)
    elif mode == "gates":
        lb_ref, k_ref, lf_ref = refs[2:]
        lb = lb_ref[...]
        f = lb + (1.0 - lb) * jax.nn.sigmoid(z)
        k_ref[...] = (1.0 - f).astype(BF16)
        lf_ref[...] = jnp.log(f)
    else:
        cos_ref, sin_ref = refs[2:4]
        outs = refs[4:]
        cos = cos_ref[...]
        sin = sin_ref[...]
        lane = lax.broadcasted_iota(jnp.int32, cos.shape, 1)
        half = ROT_DIM // 2
        for c in range(tn // LANES):
            sl = slice(c * LANES, (c + 1) * LANES)
            zc = z[:, sl]
            partner = jnp.where(lane < half, pltpu.roll(zc, LANES - half, 1), pltpu.roll(zc, half, 1))
            r = zc * cos + partner * sin
            if mode == "rope_q":
                outs[0][:, sl] = (r * Q_SCALE).astype(BF16)
            else:
                outs[0][:, sl] = r
                outs[1][:, sl] = r.astype(BF16)


def _proj(hn, w, seg, nseg, mode, out_dtypes, period, extra=()):
    m = hn.shape[0]
    tm = _tile(period, 1024, 16)
    tn = 1024
    nj = nseg * SEG // tn
    off = seg * SEG // tn
    in_specs = [pl.BlockSpec((tm, D_MODEL), lambda i, j: (i, 0)),
                pl.BlockSpec((D_MODEL, tn), lambda i, j: (0, j + off))]
    if mode == "gates":
        in_specs.append(pl.BlockSpec((1, tn), lambda i, j: (0, j)))
    elif mode in ("rope_q", "rope_k"):
        nper = period // tm
        in_specs += [pl.BlockSpec((tm, LANES), lambda i, j: (i % nper, 0))] * 2
    out_specs = [pl.BlockSpec((tm, tn), lambda i, j: (i, j)) for _ in out_dtypes]
    out_shape = [jax.ShapeDtypeStruct((m, nseg * SEG), dt) for dt in out_dtypes]
    res = pl.pallas_call(
        functools.partial(_proj_kernel, mode=mode, tn=tn),
        grid=(m // tm, nj),
        in_specs=in_specs,
        out_specs=out_specs,
        out_shape=out_shape,
        compiler_params=_params("parallel", "parallel"),
        name="in_proj_" + mode,
    )(hn, w, *extra)
    return res


def _in_proj(hn, w_bf, lb, cos_t, sin_t, period):
    (q16,) = _proj(hn, w_bf, SEG_QA, 1, "rope_q", [BF16], period, (cos_t, sin_t))
    k32, k16 = _proj(hn, w_bf, SEG_KA, 1, "rope_k", [F32, BF16], period, (cos_t, sin_t))
    v32, v16 = _proj(hn, w_bf, SEG_VA, 1, "dual", [F32, BF16], period)
    (qr16,) = _proj(hn, w_bf, SEG_QR, 1, "plain", [BF16], period)
    kr16, lf32 = _proj(hn, w_bf, SEG_FR, 1, "gates", [BF16, F32], period, (lb,))
    (ir16,) = _proj(hn, w_bf, SEG_IR, 1, "plain", [BF16], period)
    (g32,) = _proj(hn, w_bf, SEG_GR, 3, "plain", [F32], period)
    return q16, k32, k16, v32, v16, qr16, kr16, lf32, ir16, g32


def _attn_kernel(qi_ref, ki_ref, q_ref, k_ref, v_ref, km_ref, vm_ref, lam_ref, g_ref, o_ref,
                 m_ref, l_ref, acc_ref, *, tq, tk, causal, kv_valid, n_kv):
    step = pl.program_id(2)
    qi = qi_ref[step]
    ki = ki_ref[step]
    q = q_ref[0]
    piece = PV_PIECE if tk % PV_PIECE == 0 else LANES
    row_bands = 2 if tq % 512 == 0 else 1

    @pl.when(ki == 0)
    def _meta():
        col = lax.broadcasted_iota(jnp.int32, (tq, LANES), 1)
        for c in range(2):
            sl = slice(c * DK_A, (c + 1) * DK_A)
            s = lax.dot_general(q[:, sl], km_ref[:, sl], NT_DIMS, preferred_element_type=F32)
            s = jnp.where(col < N_META, s, -jnp.inf)
            m = jnp.broadcast_to(jnp.max(s, axis=1, keepdims=True), (tq, LANES))
            p = jnp.exp2(s - m)
            m_ref[c] = m
            l_ref[c] = p
            acc_ref[c] = jnp.dot(p.astype(BF16), vm_ref[...], preferred_element_type=F32)

    def chain(c, rows, mask):
        sl = slice(c * DK_A, (c + 1) * DK_A)
        s = lax.dot_general(q_ref[0, rows, sl], k_ref[0, :, sl], NT_DIMS, preferred_element_type=F32)
        if mask is not None:
            s = jnp.where(mask, s, -jnp.inf)
        m_old = m_ref[c, rows, :]
        m_new = jnp.maximum(m_old, jnp.max(s, axis=1, keepdims=True))
        alpha = jnp.exp2(m_old - m_new)
        lsum = alpha * l_ref[c, rows, :]
        pv = None
        for j in range(tk // piece):
            slabs = []
            for jj in range(piece // LANES):
                lo = j * piece + jj * LANES
                pj = jnp.exp2(s[:, lo:lo + LANES] - m_new)
                lsum = lsum + pj
                slabs.append(pj.astype(BF16))
            d = jnp.dot(jnp.concatenate(slabs, axis=1), v_ref[0, j * piece:(j + 1) * piece, :],
                        preferred_element_type=F32)
            pv = d if pv is None else pv + d
        l_ref[c, rows, :] = lsum
        m_ref[c, rows, :] = m_new
        acc_ref[c, rows, :] = jnp.concatenate([alpha] * (DV_A // LANES), axis=1) * acc_ref[c, rows, :] + pv

    def block(mask_fn):
        band = tq // row_bands
        for r in range(row_bands):
            for c in range(2):
                chain(c, slice(r * band, (r + 1) * band), None if mask_fn is None else mask_fn(r * band, band))

    if causal:
        @pl.when(ki != qi)
        def _full():
            block(None)

        @pl.when(ki == qi)
        def _diag():
            def chunk_mask(row0, band):
                row = lax.broadcasted_iota(jnp.int32, (band, tk), 0) + row0
                col = lax.broadcasted_iota(jnp.int32, (band, tk), 1)
                return col // CHUNK <= row // CHUNK
            block(chunk_mask)
        last = ki == qi
    else:
        def valid_mask(row0, band):
            return lax.broadcasted_iota(jnp.int32, (band, tk), 1) + ki * tk < kv_valid
        block(valid_mask)
        last = ki == n_kv - 1

    @pl.when(last)
    def _finish():
        lp = lam_ref[...]
        lam = (jnp.exp(jnp.sum(lp[0:1] * lp[1:2], axis=1, keepdims=True))
               - jnp.exp(jnp.sum(lp[2:3] * lp[3:4], axis=1, keepdims=True)) + LAM_INIT)
        l0 = jnp.sum(l_ref[0], axis=1, keepdims=True)
        l1 = jnp.sum(l_ref[1], axis=1, keepdims=True)
        o = acc_ref[0] / l0 - lam * (acc_ref[1] / l1)
        ms = jnp.mean(o * o, axis=1, keepdims=True)
        o_ref[0] = (o * lax.rsqrt(ms + LN_EPS) * g_ref[...] * (1.0 - LAM_INIT)).astype(o_ref.dtype)


def _diff_attn(q, k, v, k_meta, v_meta, lam_qk, subln_g, *, causal, kv_valid, tq, tk):
    bsz, lq, _ = q.shape
    lk = k.shape[1]
    nq, nk = lq // tq, lk // tk
    if causal:
        pairs = [(i, j) for i in range(nq) for j in range(i + 1)]
    else:
        pairs = [(i, j) for i in range(nq) for j in range(nk)]
    qi = jnp.asarray([p[0] for p in pairs], jnp.int32)
    ki = jnp.asarray([p[1] for p in pairs], jnp.int32)
    grid_spec = pltpu.PrefetchScalarGridSpec(
        num_scalar_prefetch=2,
        grid=(bsz, H_A, len(pairs)),
        in_specs=[
            pl.BlockSpec((1, tq, HEAD_A), lambda b, h, s, qi, ki: (b, qi[s], h)),
            pl.BlockSpec((1, tk, HEAD_A), lambda b, h, s, qi, ki: (b, ki[s], h)),
            pl.BlockSpec((1, tk, HEAD_A), lambda b, h, s, qi, ki: (b, ki[s], h)),
            pl.BlockSpec((LANES, HEAD_A), lambda b, h, s, qi, ki: (0, h)),
            pl.BlockSpec((LANES, HEAD_A), lambda b, h, s, qi, ki: (0, h)),
            pl.BlockSpec((4, DK_A), lambda b, h, s, qi, ki: (0, 0)),
            pl.BlockSpec((1, DV_A), lambda b, h, s, qi, ki: (0, 0)),
        ],
        out_specs=pl.BlockSpec((1, tq, HEAD_A), lambda b, h, s, qi, ki: (b, qi[s], h)),
        scratch_shapes=[pltpu.VMEM((2, tq, LANES), F32), pltpu.VMEM((2, tq, LANES), F32),
                        pltpu.VMEM((2, tq, DV_A), F32)],
    )
    return pl.pallas_call(
        functools.partial(_attn_kernel, tq=tq, tk=tk, causal=causal, kv_valid=kv_valid, n_kv=nk),
        grid_spec=grid_spec,
        out_shape=jax.ShapeDtypeStruct((bsz, lq, H_A * HEAD_A), BF16),
        compiler_params=_params("parallel", "parallel", "arbitrary"),
        name="diff_attn_causal" if causal else "diff_attn_cache",
    )(qi, ki, q, k, v, k_meta, v_meta, lam_qk, subln_g)


def _hgrn_kernel(q_ref, k_ref, lf_ref, v_ref, gr_ref, s0_ref, g_ref, o_ref, sout_ref, st_ref, *, tb):
    t = pl.program_id(2)

    @pl.when(t == 0)
    def _load_state():
        st_ref[...] = s0_ref[0, 0].T

    ones = jnp.ones((DK_R, LANES), BF16)
    row = lax.broadcasted_iota(jnp.int32, (CHUNK, DK_R), 0)
    sub = row // SUB
    tin = row % SUB
    g = g_ref[...]
    nsub = CHUNK // SUB

    def chunk(c, carry):
        sl = pl.ds(pl.multiple_of(c * CHUNK, CHUNK), CHUNK)
        qf = q_ref[0, sl, :].astype(F32)
        kf = k_ref[0, sl, :].astype(F32)
        vb = v_ref[0, sl, :]
        vf = vb.astype(F32)
        b = lf_ref[0, sl, :]
        sh = 1
        while sh < CHUNK:
            b = b + jnp.where(row >= sh, pltpu.roll(b, sh, 0), 0.0)
            sh *= 2
        b_last = b[CHUNK - 1:CHUNK, :]
        st = st_ref[...]
        qdec = (qf * jnp.exp(b)).astype(BF16)
        o = lax.dot_general(qdec, st.astype(BF16), NT_DIMS, preferred_element_type=F32)
        ends = [b[SUB * j + SUB - 1:SUB * j + SUB, :] for j in range(nsub)]
        end_of_sub = jnp.concatenate([jnp.broadcast_to(e, (SUB, DK_R)) for e in ends], axis=0)
        kk = kf * jnp.exp(end_of_sub - b)
        qs, ks = [], []
        for j in range(nsub - 1):
            e = jnp.exp(jnp.where(sub > j, b - ends[j], -jnp.inf))
            qs.append((qf * e).astype(BF16))
            ks.append(jnp.where(sub == j, kk, 0.0).astype(BF16))
        sc = lax.dot_general(jnp.concatenate(qs, axis=1), jnp.concatenate(ks, axis=1), NT_DIMS,
                             preferred_element_type=F32)
        o = o + jnp.dot(sc.astype(BF16), vb, preferred_element_type=F32)
        k3 = kf.reshape(nsub, SUB, DK_R)
        b3 = b.reshape(nsub, SUB, DK_R)
        v3 = vf.reshape(nsub, SUB, DV_R)
        for s in range(SUB):
            shp = (nsub, SUB, DK_R)
            ks_b = jnp.broadcast_to(k3[:, s:s + 1, :], shp).reshape(CHUNK, DK_R)
            bs_b = jnp.broadcast_to(b3[:, s:s + 1, :], shp).reshape(CHUNK, DK_R)
            vs_b = jnp.broadcast_to(v3[:, s:s + 1, :], shp).reshape(CHUNK, DV_R)
            a = qf * ks_b * jnp.exp(jnp.where(tin >= s, b - bs_b, -jnp.inf))
            o = o + jnp.dot(a.astype(BF16), ones, preferred_element_type=F32) * vs_b
        kdec = (kf * jnp.exp(b_last - b)).astype(BF16)
        st_ref[...] = jnp.exp(b_last) * st + lax.dot_general(vb, kdec, TN_DIMS, preferred_element_type=F32)
        ms = jnp.mean(o * o, axis=1, keepdims=True)
        gr = gr_ref[0, sl, :]
        o_ref[0, sl, :] = (o * lax.rsqrt(ms + LN_EPS) * g * (gr * jax.nn.sigmoid(gr))).astype(o_ref.dtype)
        return carry

    lax.fori_loop(0, tb // CHUNK, chunk, 0)

    @pl.when(t == pl.num_programs(2) - 1)
    def _store_state():
        sout_ref[0, 0] = st_ref[...].T


def _hgrn(q, k, lf, v, g3, s0, rnorm_g):
    bsz, length, _ = q.shape
    tb = _tile(length, 512, CHUNK)
    tok = lambda b, h, t: (b, t, h)
    return pl.pallas_call(
        functools.partial(_hgrn_kernel, tb=tb),
        grid=(bsz, H_R, length // tb),
        in_specs=[pl.BlockSpec((1, tb, DK_R), tok), pl.BlockSpec((1, tb, DK_R), tok),
                  pl.BlockSpec((1, tb, DK_R), tok), pl.BlockSpec((1, tb, DV_R), tok),
                  pl.BlockSpec((1, tb, DV_R), tok),
                  pl.BlockSpec((1, 1, DK_R, DV_R), lambda b, h, t: (b, h, 0, 0)),
                  pl.BlockSpec((1, DV_R), lambda b, h, t: (0, 0))],
        out_specs=[pl.BlockSpec((1, tb, DV_R), tok),
                   pl.BlockSpec((1, 1, DK_R, DV_R), lambda b, h, t: (b, h, 0, 0))],
        out_shape=[jax.ShapeDtypeStruct((bsz, length, D_MODEL), BF16),
                   jax.ShapeDtypeStruct((bsz, H_R, DK_R, DV_R), F32)],
        scratch_shapes=[pltpu.VMEM((DV_R, DK_R), F32)],
        compiler_params=_params("parallel", "parallel", "arbitrary"),
        name="hgrn2",
    )(q, k, lf, v, g3, s0, rnorm_g)


def _merge_kernel(a_ref, r_ref, wa_ref, wb_ref, ga_ref, gb_ref, o_ref):
    ya = jnp.dot(a_ref[...], wa_ref[...], preferred_element_type=F32)
    yr = jnp.dot(r_ref[...], wb_ref[...], preferred_element_type=F32)
    o_ref[...] = (jax.nn.sigmoid(ga_ref[...]) * ya + jax.nn.sigmoid(gb_ref[...]) * yr).astype(o_ref.dtype)


def _merge(a_n, r_n, w_pa, w_pb, g3):
    m = a_n.shape[0]
    tm = _tile(m, 512, 16)
    tn = 1024
    nj = D_MODEL // tn
    return pl.pallas_call(
        _merge_kernel,
        grid=(m // tm, nj),
        in_specs=[pl.BlockSpec((tm, D_MODEL), lambda i, j: (i, 0)),
                  pl.BlockSpec((tm, D_MODEL), lambda i, j: (i, 0)),
                  pl.BlockSpec((D_MODEL, tn), lambda i, j: (0, j)),
                  pl.BlockSpec((D_MODEL, tn), lambda i, j: (0, j)),
                  pl.BlockSpec((tm, tn), lambda i, j: (i, j + nj)),
                  pl.BlockSpec((tm, tn), lambda i, j: (i, j + 2 * nj))],
        out_specs=pl.BlockSpec((tm, tn), lambda i, j: (i, j)),
        out_shape=jax.ShapeDtypeStruct((m, D_MODEL), BF16),
        compiler_params=_params("parallel", "parallel"),
        name="merge",
    )(a_n, r_n, w_pa, w_pb, g3, g3)


def _post_kernel(m_ref, wo_ref, x_ref, g0_ref, b0_ref, g1_ref, b1_ref, wr_ref, br_ref,
                 h32_ref, hp_ref, te_ref, tg_ref):
    mix = jnp.dot(m_ref[...], wo_ref[...], preferred_element_type=F32)
    x = _layer_norm(x_ref[...], g0_ref[...], b0_ref[...])
    h = _layer_norm(DEEPNORM_ALPHA * x + mix, g1_ref[...], b1_ref[...])
    h32_ref[...] = h
    _store_packed(hp_ref, h, h.shape[0])
    logits = jnp.dot(h, wr_ref[...], preferred_element_type=F32, precision=lax.Precision.HIGHEST)
    logits = logits + br_ref[...]
    col = lax.broadcasted_iota(jnp.int32, logits.shape, 1)
    vals, idxs = [], []
    for _ in range(TOP_K):
        mx = jnp.max(logits, axis=1, keepdims=True)
        idx = jnp.min(jnp.where(logits == mx, col, LANES), axis=1, keepdims=True)
        vals.append(mx)
        idxs.append(idx)
        logits = jnp.where(col == idx, -jnp.inf, logits)
    es = [jnp.exp(v - vals[0]) for v in vals]
    den = es[0] + es[1] + es[2] + es[3]
    te = jnp.zeros(col.shape, jnp.int32)
    tg = jnp.zeros(col.shape, F32)
    for kk in range(TOP_K):
        te = jnp.where(col == kk, idxs[kk], te)
        tg = jnp.where(col == kk, es[kk] / den, tg)
    te_ref[...] = te
    tg_ref[...] = tg


def _post(mrg, w_o, x, ln0_g, ln0_b, ln1_g, ln1_b, w_r, b_r):
    m = mrg.shape[0]
    tm = _tile(m, 256, 16)
    row = lambda i: (i, 0)
    fix = lambda i: (0, 0)
    return pl.pallas_call(
        _post_kernel,
        grid=(m // tm,),
        in_specs=[pl.BlockSpec((tm, D_MODEL), row), pl.BlockSpec((D_MODEL, D_MODEL), fix),
                  pl.BlockSpec((tm, D_MODEL), row),
                  pl.BlockSpec((1, D_MODEL), fix), pl.BlockSpec((1, D_MODEL), fix),
                  pl.BlockSpec((1, D_MODEL), fix), pl.BlockSpec((1, D_MODEL), fix),
                  pl.BlockSpec((D_MODEL, LANES), fix), pl.BlockSpec((1, LANES), fix)],
        out_specs=[pl.BlockSpec((tm, D_MODEL), row), pl.BlockSpec((tm * ROW_SUB, LANES), row),
                   pl.BlockSpec((tm, LANES), row), pl.BlockSpec((tm, LANES), row)],
        out_shape=[jax.ShapeDtypeStruct((m, D_MODEL), F32), jax.ShapeDtypeStruct((m * ROW_SUB, LANES), jnp.uint32),
                   jax.ShapeDtypeStruct((m, LANES), jnp.int32), jax.ShapeDtypeStruct((m, LANES), F32)],
        compiler_params=_params("parallel"),
        name="post",
    )(mrg, w_o, x, ln0_g, ln0_b, ln1_g, ln1_b, w_r, b_r)


def _deint_kernel(w_ref, p_ref, g_ref, l_ref, *, tc):
    y = jnp.dot(w_ref[0].astype(BF16), p_ref[...], preferred_element_type=F32)
    g_ref[0] = y[:, :tc].astype(BF16)
    l_ref[0] = y[:, tc:].astype(BF16)


def _deinterleave(w1):
    tc, tr = 512, 1024
    r = lax.broadcasted_iota(jnp.int32, (2 * tc, 2 * tc), 0)
    c = lax.broadcasted_iota(jnp.int32, (2 * tc, 2 * tc), 1)
    perm = jnp.where(c < tc, r == 2 * c, r == 2 * (c - tc) + 1).astype(BF16)
    out = jax.ShapeDtypeStruct((N_EXPERTS, D_MODEL, D_FF), BF16)
    return pl.pallas_call(
        functools.partial(_deint_kernel, tc=tc),
        grid=(N_EXPERTS, D_MODEL // tr, D_FF // tc),
        in_specs=[pl.BlockSpec((1, tr, 2 * tc), lambda e, i, j: (e, i, j)),
                  pl.BlockSpec((2 * tc, 2 * tc), lambda e, i, j: (0, 0))],
        out_specs=[pl.BlockSpec((1, tr, tc), lambda e, i, j: (e, i, j)),
                   pl.BlockSpec((1, tr, tc), lambda e, i, j: (e, i, j))],
        out_shape=[out, out],
        compiler_params=_params("parallel", "parallel", "parallel"),
        name="w1_deinterleave",
    )(w1, perm)


ROW_SUB = 8
HALF = D_MODEL // 2


def _store_packed(o_ref, x, n):
    bits = lax.bitcast_convert_type(x.astype(BF16).astype(F32), jnp.uint32)
    for c in range(ROW_SUB):
        lo = bits[:, c * LANES:(c + 1) * LANES]
        hi = bits[:, HALF + c * LANES:HALF + (c + 1) * LANES]
        o_ref[pl.ds(c, n, stride=ROW_SUB), :] = (lo >> 16) | hi


def _load_packed(x_ref, c, n):
    w = x_ref[pl.ds(c, n, stride=ROW_SUB), :]
    lo = lax.bitcast_convert_type(w << 16, F32)
    hi = lax.bitcast_convert_type(w & jnp.uint32(0xFFFF0000), F32)
    return lo, hi


def _gather_kernel(idx_ref, src_ref, o_ref, sem, *, g):
    def row_copy(r, src_row):
        return pltpu.make_async_copy(src_ref.at[pl.ds(pl.multiple_of(src_row * ROW_SUB, ROW_SUB), ROW_SUB)],
                                     o_ref.at[pl.ds(pl.multiple_of(r * ROW_SUB, ROW_SUB), ROW_SUB)], sem)

    def issue(r, carry):
        row_copy(r, idx_ref[r]).start()
        return carry

    def drain(r, carry):
        row_copy(r, 0).wait()
        return carry

    lax.fori_loop(0, g, issue, 0)
    lax.fori_loop(0, g, drain, 0)


def _gather_rows(src, idx):
    n = idx.shape[0]
    g = _tile(n, 512, 128)
    return pl.pallas_call(
        functools.partial(_gather_kernel, g=g),
        grid=(n // g,),
        in_specs=[pl.BlockSpec((g,), lambda i: (i,), memory_space=pltpu.SMEM),
                  pl.BlockSpec(memory_space=pl.ANY)],
        out_specs=pl.BlockSpec((g * ROW_SUB, LANES), lambda i: (i, 0)),
        out_shape=jax.ShapeDtypeStruct((n * ROW_SUB, LANES), jnp.uint32),
        scratch_shapes=[pltpu.SemaphoreType.DMA(())],
        compiler_params=_params("arbitrary"),
        name="gather_rows",
    )(idx, src)


def _expert_kernel(be_ref, nb_ref, x_ref, wg_ref, wl_ref, bg_ref, bl_ref, w2_ref, b2_ref, o_ref, acc_ref, x16_ref):
    i = pl.program_id(0)
    f = pl.program_id(1)
    nf = pl.num_programs(1)
    used = i < nb_ref[0]

    blk = x16_ref.shape[0]

    @pl.when(jnp.logical_and(used, f == 0))
    def _unpack():
        for c in range(ROW_SUB):
            lo, hi = _load_packed(x_ref, c, blk)
            x16_ref[:, c * LANES:(c + 1) * LANES] = lo.astype(BF16)
            x16_ref[:, HALF + c * LANES:HALF + (c + 1) * LANES] = hi.astype(BF16)

    @pl.when(used)
    def _compute():
        x = x16_ref[...]
        hg = jnp.dot(x, wg_ref[0], preferred_element_type=F32) + bg_ref[0]
        hl = jnp.dot(x, wl_ref[0], preferred_element_type=F32) + bl_ref[0]
        glu = jnp.minimum(hg, SWIGLU_LIMIT)
        lin = jnp.clip(hl, -SWIGLU_LIMIT, SWIGLU_LIMIT)
        act = (glu * jax.nn.sigmoid(SWIGLU_ALPHA * glu) * (lin + 1.0)).astype(BF16)
        y = jnp.dot(act, w2_ref[0], preferred_element_type=F32)

        @pl.when(f == 0)
        def _first():
            acc_ref[...] = y + b2_ref[0]

        @pl.when(f > 0)
        def _rest():
            acc_ref[...] += y

        @pl.when(f == nf - 1)
        def _store():
            _store_packed(o_ref, acc_ref[...], blk)

    @pl.when(jnp.logical_and(jnp.logical_not(used), f == nf - 1))
    def _unused():
        o_ref[...] = jnp.zeros(o_ref.shape, o_ref.dtype)


def _experts(xb, blk_e, n_used, w1g, w1l, b1g, b1l, w2, b2, blk):
    rows = xb.shape[0] // ROW_SUB
    tf = 512
    nf = D_FF // tf
    grid_spec = pltpu.PrefetchScalarGridSpec(
        num_scalar_prefetch=2,
        grid=(rows // blk, nf),
        in_specs=[
            pl.BlockSpec((blk * ROW_SUB, LANES), lambda i, f, be, nb: (i, 0)),
            pl.BlockSpec((1, D_MODEL, tf), lambda i, f, be, nb: (be[i], 0, f)),
            pl.BlockSpec((1, D_MODEL, tf), lambda i, f, be, nb: (be[i], 0, f)),
            pl.BlockSpec((1, 1, tf), lambda i, f, be, nb: (be[i], 0, f)),
            pl.BlockSpec((1, 1, tf), lambda i, f, be, nb: (be[i], 0, f)),
            pl.BlockSpec((1, tf, D_MODEL), lambda i, f, be, nb: (be[i], f, 0)),
            pl.BlockSpec((1, 1, D_MODEL), lambda i, f, be, nb: (be[i], 0, 0)),
        ],
        out_specs=pl.BlockSpec((blk * ROW_SUB, LANES), lambda i, f, be, nb: (i, 0)),
        scratch_shapes=[pltpu.VMEM((blk, D_MODEL), F32), pltpu.VMEM((blk, D_MODEL), BF16)],
    )
    return pl.pallas_call(
        _expert_kernel,
        grid_spec=grid_spec,
        out_shape=jax.ShapeDtypeStruct((rows * ROW_SUB, LANES), jnp.uint32),
        compiler_params=_params("parallel", "arbitrary"),
        name="experts",
    )(blk_e, n_used, xb, w1g, w1l, b1g, b1l, w2, b2)


def _combine_kernel(h_ref, y_ref, tg_ref, g_ref, b_ref, o_ref):
    tm = h_ref.shape[0]
    tg = tg_ref[...]
    los, his = [], []
    for c in range(ROW_SUB):
        lo = hi = None
        for kk in range(TOP_K):
            ylo, yhi = _load_packed(y_ref.at[kk], c, tm)
            gate = tg[:, kk:kk + 1]
            lo = gate * ylo if lo is None else lo + gate * ylo
            hi = gate * yhi if hi is None else hi + gate * yhi
        los.append(lo)
        his.append(hi)
    f = jnp.concatenate(los + his, axis=1)
    o_ref[...] = _layer_norm(DEEPNORM_ALPHA * h_ref[...] + f, g_ref[...], b_ref[...])


def _combine(h32, yg, tg, ln2_g, ln2_b, row_off):
    m = h32.shape[0]
    tm = _tile(m, 256, 16)
    off = row_off // tm
    return pl.pallas_call(
        _combine_kernel,
        grid=(m // tm,),
        in_specs=[pl.BlockSpec((tm, D_MODEL), lambda i: (i, 0)),
                  pl.BlockSpec((TOP_K, tm * ROW_SUB, LANES), lambda i: (0, i + off, 0)),
                  pl.BlockSpec((tm, LANES), lambda i: (i, 0)),
                  pl.BlockSpec((1, D_MODEL), lambda i: (0, 0)),
                  pl.BlockSpec((1, D_MODEL), lambda i: (0, 0))],
        out_specs=pl.BlockSpec((tm, D_MODEL), lambda i: (i, 0)),
        out_shape=jax.ShapeDtypeStruct((m, D_MODEL), F32),
        compiler_params=_params("parallel"),
        name="combine",
    )(h32, yg, tg, ln2_g, ln2_b)


def _route(top_e, blk):
    t = top_e.shape[0]
    n = t * TOP_K
    flat_e = top_e.reshape(-1)
    order = jnp.argsort(flat_e)
    sorted_e = flat_e[order]
    counts = jnp.bincount(flat_e, length=N_EXPERTS)
    padded = (counts + blk - 1) // blk * blk
    pad_end = jnp.cumsum(padded)
    start = jnp.cumsum(counts) - counts
    dest = ((pad_end - padded)[sorted_e] + jnp.arange(n, dtype=jnp.int32) - start[sorted_e]).astype(jnp.int32)
    n_blocks = -(-n // blk) + N_EXPERTS
    row_tok = jnp.zeros((n_blocks * blk,), jnp.int32).at[dest].set((order // TOP_K).astype(jnp.int32))
    blk_e = jnp.minimum(jnp.searchsorted(pad_end, jnp.arange(n_blocks, dtype=jnp.int32) * blk, side="right"),
                        N_EXPERTS - 1).astype(jnp.int32)
    dest_flat = jnp.zeros((n,), jnp.int32).at[order].set(dest)
    n_used = (pad_end[-1:] // blk).astype(jnp.int32)
    return row_tok, blk_e, n_used, dest_flat.reshape(t, TOP_K)


def _pad_rows(a, n):
    return jnp.pad(a, ((0, n - a.shape[0]),) + ((0, 0),) * (a.ndim - 1))


def _forward(x_prompt, x_sample, cache_k, cache_v, state_hgrn, meta_tokens, ln0_g, ln0_b, w_in, lam_qk,
             subln_g, lb_logits, rnorm_g, w_pa, w_pb, w_o, ln1_g, ln1_b, ln2_g, ln2_b,
             w_router, b_router, w1, b1, w2, b2, *, expert_block, attn_tile):
    bsz, seq, _ = x_prompt.shape
    dec_b, dec_t, _ = x_sample.shape
    past = cache_k.shape[2]
    n_s = dec_b * dec_t
    n_p = bsz * seq

    row = lambda a: a.reshape(1, -1).astype(F32)
    g0, b0, g1, b1n, g2, b2n = row(ln0_g), row(ln0_b), row(ln1_g[0]), row(ln1_b[0]), row(ln2_g[0]), row(ln2_b[0])
    lb = jnp.cumsum(jax.nn.softmax(lb_logits.astype(F32), axis=0), axis=0)[0].reshape(1, -1)
    w_in16 = w_in[0].astype(BF16)
    w_pa16, w_pb16, w_o16 = w_pa[0].astype(BF16), w_pb[0].astype(BF16), w_o[0].astype(BF16)
    w_r = jnp.pad(w_router[0].astype(F32), ((0, 0), (0, LANES - N_EXPERTS)))
    b_r = jnp.pad(b_router[0].astype(F32), (0, LANES - N_EXPERTS), constant_values=-jnp.inf).reshape(1, LANES)
    w1g, w1l = _deinterleave(w1[0])
    b1g = b1[0, :, 0::2].reshape(N_EXPERTS, 1, D_FF).astype(F32)
    b1l = b1[0, :, 1::2].reshape(N_EXPERTS, 1, D_FF).astype(F32)
    w2_16 = w2[0].astype(BF16)
    b2r = b2[0].reshape(N_EXPERTS, 1, D_MODEL).astype(F32)
    lam = lam_qk[0].astype(F32)
    sub_g = subln_g[0].reshape(1, DV_A).astype(F32)
    rn_g = rnorm_g[0].reshape(1, DV_R).astype(F32)

    xs = jnp.concatenate([x_sample.reshape(n_s, D_MODEL), meta_tokens], axis=0)
    n_sm = n_s + N_META
    pos_s = jnp.concatenate([jnp.tile(N_META + past + jnp.arange(dec_t, dtype=jnp.int32), dec_b),
                             jnp.arange(N_META, dtype=jnp.int32)])
    cos_s, sin_s = _rope_tables(pos_s)
    hs16 = _ln_bf16(xs, g0, b0)
    q_s, k32_s, k16_s, v32_s, v16_s, qr_s, kr_s, lf_s, ir_s, g3_s = _in_proj(hs16, w_in16, lb, cos_s, sin_s, n_sm)
    k_meta32, v_meta32 = k32_s[n_s:], v32_s[n_s:]
    k_meta16 = _pad_rows(k16_s[n_s:], LANES)
    v_meta16 = _pad_rows(v16_s[n_s:], LANES)

    lk = past + dec_t
    lk_pad = -(-lk // LANES) * LANES
    kv_pad = ((0, 0), (0, lk_pad - lk), (0, 0))
    k_cat = jnp.pad(jnp.concatenate([cache_k[0].reshape(dec_b, past, D_MODEL).astype(BF16),
                                     k16_s[:n_s].reshape(dec_b, dec_t, D_MODEL)], axis=1), kv_pad)
    v_cat = jnp.pad(jnp.concatenate([cache_v[0].reshape(dec_b, past, D_MODEL).astype(BF16),
                                     v16_s[:n_s].reshape(dec_b, dec_t, D_MODEL)], axis=1), kv_pad)
    a_s = _diff_attn(q_s[:n_s].reshape(dec_b, dec_t, D_MODEL), k_cat, v_cat, k_meta16, v_meta16, lam, sub_g,
                     causal=False, kv_valid=lk, tq=dec_t, tk=_tile(lk_pad, 2048, LANES))

    n_str = dec_b + 1
    chunk_pad = ((0, 0), (0, CHUNK - dec_t), (0, 0))
    to_chunk = lambda a: jnp.pad(a.reshape(n_str, dec_t, a.shape[-1]), chunk_pad)
    s0 = jnp.concatenate([state_hgrn[0].astype(F32), jnp.zeros((1, H_R, DK_R, DV_R), F32)], axis=0)
    r_s, s_out = _hgrn(to_chunk(qr_s), to_chunk(kr_s), to_chunk(lf_s), to_chunk(ir_s), to_chunk(g3_s), s0, rn_g)
    s_sample, s_meta = s_out[:dec_b], s_out[dec_b:]

    mrg_s = _merge(a_s.reshape(n_s, D_MODEL), r_s[:dec_b, :dec_t].reshape(n_s, D_MODEL), w_pa16, w_pb16, g3_s[:n_s])
    h32_s, hp_s, te_s, tg_s = _post(mrg_s, w_o16, x_sample.reshape(n_s, D_MODEL), g0, b0, g1, b1n, w_r, b_r)

    xp = x_prompt.reshape(n_p, D_MODEL)
    cos_p, sin_p = _rope_tables(N_META + jnp.arange(seq, dtype=jnp.int32))
    hp16 = _ln_bf16(xp, g0, b0)
    q_p, k32_p, k16_p, v32_p, v16_p, qr_p, kr_p, lf_p, ir_p, g3_p = _in_proj(hp16, w_in16, lb, cos_p, sin_p, seq)
    t3 = lambda a: a.reshape(bsz, seq, a.shape[-1])
    ta = _tile(seq, attn_tile, CHUNK)
    a_p = _diff_attn(t3(q_p), t3(k16_p), t3(v16_p), k_meta16, v_meta16, lam, sub_g,
                     causal=True, kv_valid=seq, tq=ta, tk=ta)
    r_p, s_prompt = _hgrn(t3(qr_p), t3(kr_p), t3(lf_p), t3(ir_p), t3(g3_p),
                          jnp.broadcast_to(s_meta, (bsz, H_R, DK_R, DV_R)), rn_g)
    mrg_p = _merge(a_p.reshape(n_p, D_MODEL), r_p.reshape(n_p, D_MODEL), w_pa16, w_pb16, g3_p)
    h32_p, hp_p, te_p, tg_p = _post(mrg_p, w_o16, xp, g0, b0, g1, b1n, w_r, b_r)

    hp = jnp.concatenate([hp_p, hp_s], axis=0)
    top_e = jnp.concatenate([te_p[:, :TOP_K], te_s[:, :TOP_K]], axis=0)
    row_tok, blk_e, n_used, dest = _route(top_e, expert_block)
    xb = _gather_rows(hp, row_tok)
    yb = _experts(xb, blk_e, n_used, w1g, w1l, b1g, b1l, w2_16, b2r, expert_block)
    yg = _gather_rows(yb, dest.T.reshape(-1)).reshape(TOP_K, (n_p + n_s) * ROW_SUB, LANES)
    y_p = _combine(h32_p, yg, tg_p, g2, b2n, 0)
    y_s = _combine(h32_s, yg, tg_s, g2, b2n, n_p)

    k_prompt = jnp.concatenate([jnp.broadcast_to(k_meta32[None], (bsz, N_META, D_MODEL)), t3(k32_p)], axis=1)
    v_prompt = jnp.concatenate([jnp.broadcast_to(v_meta32[None], (bsz, N_META, D_MODEL)), t3(v32_p)], axis=1)
    return (y_p.reshape(bsz, seq, D_MODEL),
            y_s.reshape(dec_b, dec_t, D_MODEL),
            k_prompt.reshape(1, bsz, N_META + seq, H_A, 2, DK_A),
            v_prompt.reshape(1, bsz, N_META + seq, H_A, DV_A),
            s_prompt[None],
            k32_s[:n_s].reshape(1, dec_b, dec_t, H_A, 2, DK_A),
            v32_s[:n_s].reshape(1, dec_b, dec_t, H_A, DV_A),
            s_sample[None])


def kernel(x_prompt, x_sample, cache_k, cache_v, state_hgrn, meta_tokens, ln0_g, ln0_b, w_in, lam_qk, subln_g, lb_logits, rnorm_g, w_pa, w_pb, w_o, ln1_g, ln1_b, ln2_g, ln2_b, w_router, b_router, w1, b1, w2, b2):
    return _forward(x_prompt, x_sample, cache_k, cache_v, state_hgrn, meta_tokens, ln0_g, ln0_b, w_in, lam_qk,
                    subln_g, lb_logits, rnorm_g, w_pa, w_pb, w_o, ln1_g, ln1_b, ln2_g, ln2_b,
                    w_router, b_router, w1, b1, w2, b2, expert_block=512, attn_tile=1024)
```

```python
import functools
import math

import jax
import jax.numpy as jnp
from jax import lax
from jax.experimental import pallas as pl
from jax.experimental.pallas import tpu as pltpu

F32 = jnp.float32
BF16 = jnp.bfloat16

D_MODEL = 2048
CHUNK = 64
SUB = 16
N_META = 16
H_A = 8
DK_A = 128
DV_A = 2 * DK_A
HEAD_A = 2 * DK_A
ROT_DIM = DK_A // 4
ROPE_THETA = 500000.0
H_R = 16
DK_R = D_MODEL // H_R
DV_R = D_MODEL // H_R
N_EXPERTS = 32
TOP_K = 4
D_FF = D_MODEL
SWIGLU_ALPHA = 1.702
SWIGLU_LIMIT = 7.0
LN_EPS = 1e-5
DEPTH = 1
DEEPNORM_ALPHA = (2.0 * DEPTH) ** 0.25
LAM_INIT = 0.8 - 0.6 * math.exp(-0.3 * 0)

SEG = D_MODEL
SEG_QA, SEG_KA, SEG_VA, SEG_QR, SEG_FR, SEG_IR, SEG_GR, SEG_GA, SEG_GB = range(9)

LOG2_E = math.log2(math.e)
Q_SCALE = DK_A ** -0.5 * LOG2_E
PV_PIECE = 256
HGRN_UNROLL = 8

LANES = 128
VMEM_LIMIT_BYTES = 56 * 1024 * 1024

NT_DIMS = (((1,), (1,)), ((), ()))
TN_DIMS = (((0,), (0,)), ((), ()))


def _params(*sem):
    return pltpu.CompilerParams(dimension_semantics=sem, vmem_limit_bytes=VMEM_LIMIT_BYTES)


def _tile(n, pref, mult):
    if n <= pref:
        return n
    t = pref - pref % mult
    while t > mult and n % t:
        t -= mult
    assert n % t == 0, (n, pref, mult)
    return t


def _layer_norm(x, g, b):
    mu = jnp.mean(x, axis=-1, keepdims=True)
    xc = x - mu
    var = jnp.mean(xc * xc, axis=-1, keepdims=True)
    return xc * lax.rsqrt(var + LN_EPS) * g + b


def _ln_kernel(x_ref, g_ref, b_ref, o_ref):
    o_ref[...] = _layer_norm(x_ref[...], g_ref[...], b_ref[...]).astype(o_ref.dtype)


def _ln_bf16(x, g, b):
    m = x.shape[0]
    tm = _tile(m, 512, 16)
    return pl.pallas_call(
        _ln_kernel,
        grid=(m // tm,),
        in_specs=[pl.BlockSpec((tm, D_MODEL), lambda i: (i, 0)),
                  pl.BlockSpec((1, D_MODEL), lambda i: (0, 0)),
                  pl.BlockSpec((1, D_MODEL), lambda i: (0, 0))],
        out_specs=pl.BlockSpec((tm, D_MODEL), lambda i: (i, 0)),
        out_shape=jax.ShapeDtypeStruct((m, D_MODEL), BF16),
        compiler_params=_params("parallel"),
        name="ln0",
    )(x, g, b)


def _rope_tables(pos):
    inv_freq = ROPE_THETA ** (-jnp.arange(0, ROT_DIM, 2, dtype=F32) / ROT_DIM)
    ang = pos.astype(F32)[:, None] * inv_freq[None, :]
    n = pos.shape[0]
    cos, sin = jnp.cos(ang), jnp.sin(ang)
    rest = LANES - ROT_DIM
    cos_t = jnp.concatenate([cos, cos, jnp.ones((n, rest), F32)], axis=1)
    sin_t = jnp.concatenate([-sin, sin, jnp.zeros((n, rest), F32)], axis=1)
    return cos_t, sin_t


def _proj_kernel(*refs, mode, tn):
    x_ref, w_ref = refs[:2]
    z = jnp.dot(x_ref[...], w_ref[...], preferred_element_type=F32)
    if mode == "plain":
        (o_ref,) = refs[2:]
        o_ref[...] = z.astype(o_ref.dtype)
    elif mode == "dual":
        o32_ref, o16_ref = refs[2:]
        o32_ref[...] = z
        o16_ref[...] = z.astype(BF16)
    elif mode == "gates":
        lb_ref, k_ref, lf_ref = refs[2:]
        lb = lb_ref[...]
        f = lb + (1.0 - lb) * jax.nn.sigmoid(z)
        k_ref[...] = (1.0 - f).astype(BF16)
        lf_ref[...] = jnp.log(f)
    else:
        cos_ref, sin_ref = refs[2:4]
        outs = refs[4:]
        cos = cos_ref[...]
        sin = sin_ref[...]
        lane = lax.broadcasted_iota(jnp.int32, cos.shape, 1)
        half = ROT_DIM // 2
        for c in range(tn // LANES):
            sl = slice(c * LANES, (c + 1) * LANES)
            zc = z[:, sl]
            partner = jnp.where(lane < half, pltpu.roll(zc, LANES - half, 1), pltpu.roll(zc, half, 1))
            r = zc * cos + partner * sin
            if mode == "rope_q":
                outs[0][:, sl] = (r * Q_SCALE).astype(BF16)
            else:
                outs[0][:, sl] = r
                outs[1][:, sl] = r.astype(BF16)


def _proj(hn, w, seg, nseg, mode, out_dtypes, period, extra=()):
    m = hn.shape[0]
    tm = _tile(period, 1024, 16)
    tn = 1024
    nj = nseg * SEG // tn
    off = seg * SEG // tn
    in_specs = [pl.BlockSpec((tm, D_MODEL), lambda i, j: (i, 0)),
                pl.BlockSpec((D_MODEL, tn), lambda i, j: (0, j + off))]
    if mode == "gates":
        in_specs.append(pl.BlockSpec((1, tn), lambda i, j: (0, j)))
    elif mode in ("rope_q", "rope_k"):
        nper = period // tm
        in_specs += [pl.BlockSpec((tm, LANES), lambda i, j: (i % nper, 0))] * 2
    out_specs = [pl.BlockSpec((tm, tn), lambda i, j: (i, j)) for _ in out_dtypes]
    out_shape = [jax.ShapeDtypeStruct((m, nseg * SEG), dt) for dt in out_dtypes]
    res = pl.pallas_call(
        functools.partial(_proj_kernel, mode=mode, tn=tn),
        grid=(m // tm, nj),
        in_specs=in_specs,
        out_specs=out_specs,
        out_shape=out_shape,
        compiler_params=_params("parallel", "parallel"),
        name="in_proj_" + mode,
    )(hn, w, *extra)
    return res


def _in_proj(hn, w_bf, lb, cos_t, sin_t, period):
    (q16,) = _proj(hn, w_bf, SEG_QA, 1, "rope_q", [BF16], period, (cos_t, sin_t))
    k32, k16 = _proj(hn, w_bf, SEG_KA, 1, "rope_k", [F32, BF16], period, (cos_t, sin_t))
    v32, v16 = _proj(hn, w_bf, SEG_VA, 1, "dual", [F32, BF16], period)
    (qr16,) = _proj(hn, w_bf, SEG_QR, 1, "plain", [BF16], period)
    kr16, lf32 = _proj(hn, w_bf, SEG_FR, 1, "gates", [BF16, F32], period, (lb,))
    (ir16,) = _proj(hn, w_bf, SEG_IR, 1, "plain", [BF16], period)
    (g32,) = _proj(hn, w_bf, SEG_GR, 3, "plain", [F32], period)
    return q16, k32, k16, v32, v16, qr16, kr16, lf32, ir16, g32


def _attn_kernel(qi_ref, ki_ref, q_ref, k_ref, v_ref, km_ref, vm_ref, lam_ref, g_ref, o_ref,
                 m_ref, l_ref, acc_ref, *, tq, tk, causal, kv_valid, n_kv):
    step = pl.program_id(2)
    qi = qi_ref[step]
    ki = ki_ref[step]
    q = q_ref[0]
    piece = PV_PIECE if tk % PV_PIECE == 0 else LANES
    row_bands = 2 if tq % 512 == 0 else 1

    @pl.when(ki == 0)
    def _meta():
        col = lax.broadcasted_iota(jnp.int32, (tq, LANES), 1)
        for c in range(2):
            sl = slice(c * DK_A, (c + 1) * DK_A)
            s = lax.dot_general(q[:, sl], km_ref[:, sl], NT_DIMS, preferred_element_type=F32)
            s = jnp.where(col < N_META, s, -jnp.inf)
            m = jnp.broadcast_to(jnp.max(s, axis=1, keepdims=True), (tq, LANES))
            p = jnp.exp2(s - m)
            m_ref[c] = m
            l_ref[c] = p
            acc_ref[c] = jnp.dot(p.astype(BF16), vm_ref[...], preferred_element_type=F32)

    def chain(c, rows, mask):
        sl = slice(c * DK_A, (c + 1) * DK_A)
        s = lax.dot_general(q_ref[0, rows, sl], k_ref[0, :, sl], NT_DIMS, preferred_element_type=F32)
        if mask is not None:
            s = jnp.where(mask, s, -jnp.inf)
        m_old = m_ref[c, rows, :]
        m_new = jnp.maximum(m_old, jnp.max(s, axis=1, keepdims=True))
        alpha = jnp.exp2(m_old - m_new)
        lsum = alpha * l_ref[c, rows, :]
        pv = None
        for j in range(tk // piece):
            slabs = []
            for jj in range(piece // LANES):
                lo = j * piece + jj * LANES
                pj = jnp.exp2(s[:, lo:lo + LANES] - m_new)
                lsum = lsum + pj
                slabs.append(pj.astype(BF16))
            d = jnp.dot(jnp.concatenate(slabs, axis=1), v_ref[0, j * piece:(j + 1) * piece, :],
                        preferred_element_type=F32)
            pv = d if pv is None else pv + d
        l_ref[c, rows, :] = lsum
        m_ref[c, rows, :] = m_new
        acc_ref[c, rows, :] = jnp.concatenate([alpha] * (DV_A // LANES), axis=1) * acc_ref[c, rows, :] + pv

    def block(mask_fn):
        band = tq // row_bands
        for r in range(row_bands):
            for c in range(2):
                chain(c, slice(r * band, (r + 1) * band), None if mask_fn is None else mask_fn(r * band, band))

    if causal:
        @pl.when(ki != qi)
        def _full():
            block(None)

        @pl.when(ki == qi)
        def _diag():
            def chunk_mask(row0, band):
                row = lax.broadcasted_iota(jnp.int32, (band, tk), 0) + row0
                col = lax.broadcasted_iota(jnp.int32, (band, tk), 1)
                return col // CHUNK <= row // CHUNK
            block(chunk_mask)
        last = ki == qi
    else:
        def valid_mask(row0, band):
            return lax.broadcasted_iota(jnp.int32, (band, tk), 1) + ki * tk < kv_valid
        block(valid_mask)
        last = ki == n_kv - 1

    @pl.when(last)
    def _finish():
        lp = lam_ref[...]
        lam = (jnp.exp(jnp.sum(lp[0:1] * lp[1:2], axis=1, keepdims=True))
               - jnp.exp(jnp.sum(lp[2:3] * lp[3:4], axis=1, keepdims=True)) + LAM_INIT)
        l0 = jnp.sum(l_ref[0], axis=1, keepdims=True)
        l1 = jnp.sum(l_ref[1], axis=1, keepdims=True)
        o = acc_ref[0] / l0 - lam * (acc_ref[1] / l1)
        ms = jnp.mean(o * o, axis=1, keepdims=True)
        o_ref[0] = (o * lax.rsqrt(ms + LN_EPS) * g_ref[...] * (1.0 - LAM_INIT)).astype(o_ref.dtype)


def _diff_attn(q, k, v, k_meta, v_meta, lam_qk, subln_g, *, causal, kv_valid, tq, tk):
    bsz, lq, _ = q.shape
    lk = k.shape[1]
    nq, nk = lq // tq, lk // tk
    if causal:
        pairs = [(i, j) for i in range(nq) for j in range(i + 1)]
    else:
        pairs = [(i, j) for i in range(nq) for j in range(nk)]
    qi = jnp.asarray([p[0] for p in pairs], jnp.int32)
    ki = jnp.asarray([p[1] for p in pairs], jnp.int32)
    grid_spec = pltpu.PrefetchScalarGridSpec(
        num_scalar_prefetch=2,
        grid=(bsz, H_A, len(pairs)),
        in_specs=[
            pl.BlockSpec((1, tq, HEAD_A), lambda b, h, s, qi, ki: (b, qi[s], h)),
            pl.BlockSpec((1, tk, HEAD_A), lambda b, h, s, qi, ki: (b, ki[s], h)),
            pl.BlockSpec((1, tk, HEAD_A), lambda b, h, s, qi, ki: (b, ki[s], h)),
            pl.BlockSpec((LANES, HEAD_A), lambda b, h, s, qi, ki: (0, h)),
            pl.BlockSpec((LANES, HEAD_A), lambda b, h, s, qi, ki: (0, h)),
            pl.BlockSpec((4, DK_A), lambda b, h, s, qi, ki: (0, 0)),
            pl.BlockSpec((1, DV_A), lambda b, h, s, qi, ki: (0, 0)),
        ],
        out_specs=pl.BlockSpec((1, tq, HEAD_A), lambda b, h, s, qi, ki: (b, qi[s], h)),
        scratch_shapes=[pltpu.VMEM((2, tq, LANES), F32), pltpu.VMEM((2, tq, LANES), F32),
                        pltpu.VMEM((2, tq, DV_A), F32)],
    )
    return pl.pallas_call(
        functools.partial(_attn_kernel, tq=tq, tk=tk, causal=causal, kv_valid=kv_valid, n_kv=nk),
        grid_spec=grid_spec,
        out_shape=jax.ShapeDtypeStruct((bsz, lq, H_A * HEAD_A), BF16),
        compiler_params=_params("parallel", "parallel", "arbitrary"),
        name="diff_attn_causal" if causal else "diff_attn_cache",
    )(qi, ki, q, k, v, k_meta, v_meta, lam_qk, subln_g)


def _hgrn_kernel(q_ref, k_ref, lf_ref, v_ref, gr_ref, s0_ref, g_ref, o_ref, sout_ref, st_ref, *, tb):
    t = pl.program_id(2)

    @pl.when(t == 0)
    def _load_state():
        st_ref[...] = s0_ref[0, 0].T

    ones = jnp.ones((DK_R, LANES), BF16)
    row = lax.broadcasted_iota(jnp.int32, (CHUNK, DK_R), 0)
    sub = row // SUB
    tin = row % SUB
    g = g_ref[...]
    nsub = CHUNK // SUB

    def chunk(c, carry):
        sl = pl.ds(pl.multiple_of(c * CHUNK, CHUNK), CHUNK)
        qf = q_ref[0, sl, :].astype(F32)
        kf = k_ref[0, sl, :].astype(F32)
        vb = v_ref[0, sl, :]
        vf = vb.astype(F32)
        b = lf_ref[0, sl, :]
        sh = 1
        while sh < CHUNK:
            b = b + jnp.where(row >= sh, pltpu.roll(b, sh, 0), 0.0)
            sh *= 2
        b = b * LOG2_E
        b_last = b[CHUNK - 1:CHUNK, :]
        st = st_ref[...]
        qdec = (qf * jnp.exp2(b)).astype(BF16)
        o = lax.dot_general(qdec, st.astype(BF16), NT_DIMS, preferred_element_type=F32)
        ends = [b[SUB * j + SUB - 1:SUB * j + SUB, :] for j in range(nsub)]
        end_of_sub = jnp.concatenate([jnp.broadcast_to(e, (SUB, DK_R)) for e in ends], axis=0)
        kk = kf * jnp.exp2(end_of_sub - b)
        qs, ks = [], []
        for j in range(nsub - 1):
            e = jnp.exp2(jnp.where(sub > j, b - ends[j], -jnp.inf))
            qs.append((qf * e).astype(BF16))
            ks.append(jnp.where(sub == j, kk, 0.0).astype(BF16))
        sc = lax.dot_general(jnp.concatenate(qs, axis=1), jnp.concatenate(ks, axis=1), NT_DIMS,
                             preferred_element_type=F32)
        o = o + jnp.dot(sc.astype(BF16), vb, preferred_element_type=F32)
        k3 = kf.reshape(nsub, SUB, DK_R)
        b3 = b.reshape(nsub, SUB, DK_R)
        v3 = vf.reshape(nsub, SUB, DV_R)
        for s in range(SUB):
            shp = (nsub, SUB, DK_R)
            ks_b = jnp.broadcast_to(k3[:, s:s + 1, :], shp).reshape(CHUNK, DK_R)
            bs_b = jnp.broadcast_to(b3[:, s:s + 1, :], shp).reshape(CHUNK, DK_R)
            vs_b = jnp.broadcast_to(v3[:, s:s + 1, :], shp).reshape(CHUNK, DV_R)
            db = b - bs_b
            a = qf * ks_b * jnp.exp2(db if s == 0 else jnp.where(tin >= s, db, -jnp.inf))
            o = o + jnp.dot(a.astype(BF16), ones, preferred_element_type=F32) * vs_b
        kdec = (kf * jnp.exp2(b_last - b)).astype(BF16)
        st_ref[...] = jnp.exp2(b_last) * st + lax.dot_general(vb, kdec, TN_DIMS, preferred_element_type=F32)
        ms = jnp.mean(o * o, axis=1, keepdims=True)
        gr = gr_ref[0, sl, :]
        o_ref[0, sl, :] = (o * lax.rsqrt(ms + LN_EPS) * g * (gr * jax.nn.sigmoid(gr))).astype(o_ref.dtype)
        return carry

    lax.fori_loop(0, tb // CHUNK, chunk, 0, unroll=min(HGRN_UNROLL, tb // CHUNK))

    @pl.when(t == pl.num_programs(2) - 1)
    def _store_state():
        sout_ref[0, 0] = st_ref[...].T


def _hgrn(q, k, lf, v, g3, s0, rnorm_g):
    bsz, length, _ = q.shape
    tb = _tile(length, 1024, CHUNK)
    tok = lambda b, h, t: (b, t, h)
    return pl.pallas_call(
        functools.partial(_hgrn_kernel, tb=tb),
        grid=(bsz, H_R, length // tb),
        in_specs=[pl.BlockSpec((1, tb, DK_R), tok), pl.BlockSpec((1, tb, DK_R), tok),
                  pl.BlockSpec((1, tb, DK_R), tok), pl.BlockSpec((1, tb, DV_R), tok),
                  pl.BlockSpec((1, tb, DV_R), tok),
                  pl.BlockSpec((1, 1, DK_R, DV_R), lambda b, h, t: (b, h, 0, 0)),
                  pl.BlockSpec((1, DV_R), lambda b, h, t: (0, 0))],
        out_specs=[pl.BlockSpec((1, tb, DV_R), tok),
                   pl.BlockSpec((1, 1, DK_R, DV_R), lambda b, h, t: (b, h, 0, 0))],
        out_shape=[jax.ShapeDtypeStruct((bsz, length, D_MODEL), BF16),
                   jax.ShapeDtypeStruct((bsz, H_R, DK_R, DV_R), F32)],
        scratch_shapes=[pltpu.VMEM((DV_R, DK_R), F32)],
        compiler_params=_params("parallel", "parallel", "arbitrary"),
        name="hgrn2",
    )(q, k, lf, v, g3, s0, rnorm_g)


def _merge_kernel(a_ref, r_ref, wa_ref, wb_ref, ga_ref, gb_ref, o_ref):
    ya = jnp.dot(a_ref[...], wa_ref[...], preferred_element_type=F32)
    yr = jnp.dot(r_ref[...], wb_ref[...], preferred_element_type=F32)
    o_ref[...] = (jax.nn.sigmoid(ga_ref[...]) * ya + jax.nn.sigmoid(gb_ref[...]) * yr).astype(o_ref.dtype)


def _merge(a_n, r_n, w_pa, w_pb, g3):
    m = a_n.shape[0]
    tm = _tile(m, 512, 16)
    tn = 1024
    nj = D_MODEL // tn
    return pl.pallas_call(
        _merge_kernel,
        grid=(m // tm, nj),
        in_specs=[pl.BlockSpec((tm, D_MODEL), lambda i, j: (i, 0)),
                  pl.BlockSpec((tm, D_MODEL), lambda i, j: (i, 0)),
                  pl.BlockSpec((D_MODEL, tn), lambda i, j: (0, j)),
                  pl.BlockSpec((D_MODEL, tn), lambda i, j: (0, j)),
                  pl.BlockSpec((tm, tn), lambda i, j: (i, j + nj)),
                  pl.BlockSpec((tm, tn), lambda i, j: (i, j + 2 * nj))],
        out_specs=pl.BlockSpec((tm, tn), lambda i, j: (i, j)),
        out_shape=jax.ShapeDtypeStruct((m, D_MODEL), BF16),
        compiler_params=_params("parallel", "parallel"),
        name="merge",
    )(a_n, r_n, w_pa, w_pb, g3, g3)


def _post_kernel(m_ref, wo_ref, x_ref, g0_ref, b0_ref, g1_ref, b1_ref, wr_ref, br_ref,
                 h32_ref, hp_ref, te_ref, tg_ref):
    mix = jnp.dot(m_ref[...], wo_ref[...], preferred_element_type=F32)
    x = _layer_norm(x_ref[...], g0_ref[...], b0_ref[...])
    h = _layer_norm(DEEPNORM_ALPHA * x + mix, g1_ref[...], b1_ref[...])
    h32_ref[...] = h
    _store_packed(hp_ref, h, h.shape[0])
    logits = jnp.dot(h, wr_ref[...], preferred_element_type=F32, precision=lax.Precision.HIGHEST)
    logits = logits + br_ref[...]
    col = lax.broadcasted_iota(jnp.int32, logits.shape, 1)
    vals, idxs = [], []
    for _ in range(TOP_K):
        mx = jnp.max(logits, axis=1, keepdims=True)
        idx = jnp.min(jnp.where(logits == mx, col, LANES), axis=1, keepdims=True)
        vals.append(mx)
        idxs.append(idx)
        logits = jnp.where(col == idx, -jnp.inf, logits)
    es = [jnp.exp(v - vals[0]) for v in vals]
    den = es[0] + es[1] + es[2] + es[3]
    te = jnp.zeros(col.shape, jnp.int32)
    tg = jnp.zeros(col.shape, F32)
    for kk in range(TOP_K):
        te = jnp.where(col == kk, idxs[kk], te)
        tg = jnp.where(col == kk, es[kk] / den, tg)
    te_ref[...] = te
    tg_ref[...] = tg


def _post(mrg, w_o, x, ln0_g, ln0_b, ln1_g, ln1_b, w_r, b_r):
    m = mrg.shape[0]
    tm = _tile(m, 256, 16)
    row = lambda i: (i, 0)
    fix = lambda i: (0, 0)
    return pl.pallas_call(
        _post_kernel,
        grid=(m // tm,),
        in_specs=[pl.BlockSpec((tm, D_MODEL), row), pl.BlockSpec((D_MODEL, D_MODEL), fix),
                  pl.BlockSpec((tm, D_MODEL), row),
                  pl.BlockSpec((1, D_MODEL), fix), pl.BlockSpec((1, D_MODEL), fix),
                  pl.BlockSpec((1, D_MODEL), fix), pl.BlockSpec((1, D_MODEL), fix),
                  pl.BlockSpec((D_MODEL, LANES), fix), pl.BlockSpec((1, LANES), fix)],
        out_specs=[pl.BlockSpec((tm, D_MODEL), row), pl.BlockSpec((tm * ROW_SUB, LANES), row),
                   pl.BlockSpec((tm, LANES), row), pl.BlockSpec((tm, LANES), row)],
        out_shape=[jax.ShapeDtypeStruct((m, D_MODEL), F32), jax.ShapeDtypeStruct((m * ROW_SUB, LANES), jnp.uint32),
                   jax.ShapeDtypeStruct((m, LANES), jnp.int32), jax.ShapeDtypeStruct((m, LANES), F32)],
        compiler_params=_params("parallel"),
        name="post",
    )(mrg, w_o, x, ln0_g, ln0_b, ln1_g, ln1_b, w_r, b_r)


def _deint_kernel(w_ref, p_ref, g_ref, l_ref, *, tc):
    y = jnp.dot(w_ref[0].astype(BF16), p_ref[...], preferred_element_type=F32)
    g_ref[0] = y[:, :tc].astype(BF16)
    l_ref[0] = y[:, tc:].astype(BF16)


def _deinterleave(w1):
    tc, tr = 512, 1024
    r = lax.broadcasted_iota(jnp.int32, (2 * tc, 2 * tc), 0)
    c = lax.broadcasted_iota(jnp.int32, (2 * tc, 2 * tc), 1)
    perm = jnp.where(c < tc, r == 2 * c, r == 2 * (c - tc) + 1).astype(BF16)
    out = jax.ShapeDtypeStruct((N_EXPERTS, D_MODEL, D_FF), BF16)
    return pl.pallas_call(
        functools.partial(_deint_kernel, tc=tc),
        grid=(N_EXPERTS, D_MODEL // tr, D_FF // tc),
        in_specs=[pl.BlockSpec((1, tr, 2 * tc), lambda e, i, j: (e, i, j)),
                  pl.BlockSpec((2 * tc, 2 * tc), lambda e, i, j: (0, 0))],
        out_specs=[pl.BlockSpec((1, tr, tc), lambda e, i, j: (e, i, j)),
                   pl.BlockSpec((1, tr, tc), lambda e, i, j: (e, i, j))],
        out_shape=[out, out],
        compiler_params=_params("parallel", "parallel", "parallel"),
        name="w1_deinterleave",
    )(w1, perm)


ROW_SUB = 8
HALF = D_MODEL // 2


def _store_packed(o_ref, x, n):
    bits = lax.bitcast_convert_type(x.astype(BF16).astype(F32), jnp.uint32)
    for c in range(ROW_SUB):
        lo = bits[:, c * LANES:(c + 1) * LANES]
        hi = bits[:, HALF + c * LANES:HALF + (c + 1) * LANES]
        o_ref[pl.ds(c, n, stride=ROW_SUB), :] = (lo >> 16) | hi


def _load_packed(x_ref, c, n):
    w = x_ref[pl.ds(c, n, stride=ROW_SUB), :]
    lo = lax.bitcast_convert_type(w << 16, F32)
    hi = lax.bitcast_convert_type(w & jnp.uint32(0xFFFF0000), F32)
    return lo, hi


def _gather_kernel(idx_ref, src_ref, o_ref, sem, *, g):
    def row_copy(r, src_row):
        return pltpu.make_async_copy(src_ref.at[pl.ds(pl.multiple_of(src_row * ROW_SUB, ROW_SUB), ROW_SUB)],
                                     o_ref.at[pl.ds(pl.multiple_of(r * ROW_SUB, ROW_SUB), ROW_SUB)], sem)

    def issue(r, carry):
        row_copy(r, idx_ref[r]).start()
        return carry

    def drain(r, carry):
        row_copy(r, 0).wait()
        return carry

    lax.fori_loop(0, g, issue, 0)
    lax.fori_loop(0, g, drain, 0)


def _gather_rows(src, idx):
    n = idx.shape[0]
    g = _tile(n, 512, 128)
    return pl.pallas_call(
        functools.partial(_gather_kernel, g=g),
        grid=(n // g,),
        in_specs=[pl.BlockSpec((g,), lambda i: (i,), memory_space=pltpu.SMEM),
                  pl.BlockSpec(memory_space=pl.ANY)],
        out_specs=pl.BlockSpec((g * ROW_SUB, LANES), lambda i: (i, 0)),
        out_shape=jax.ShapeDtypeStruct((n * ROW_SUB, LANES), jnp.uint32),
        scratch_shapes=[pltpu.SemaphoreType.DMA(())],
        compiler_params=_params("arbitrary"),
        name="gather_rows",
    )(idx, src)


def _expert_kernel(be_ref, nb_ref, idx0_ref, idxn_ref, hp_ref, wg_ref, wl_ref, bg_ref, bl_ref, w2_ref, b2_ref,
                   o_ref, acc_ref, x16_ref, xbuf_ref, sem, *, nf):
    i = pl.program_id(0)
    f = pl.program_id(1)
    nblk = pl.num_programs(0)
    used = i < nb_ref[0]
    blk = x16_ref.shape[0]
    per = blk // nf
    slot = i % 2
    nxt = 1 - slot

    def row_copy(s, r, src_row):
        return pltpu.make_async_copy(hp_ref.at[pl.ds(pl.multiple_of(src_row * ROW_SUB, ROW_SUB), ROW_SUB)],
                                     xbuf_ref.at[s, pl.ds(pl.multiple_of(r * ROW_SUB, ROW_SUB), ROW_SUB)],
                                     sem.at[s])

    def drain(s):
        def body(r, carry):
            row_copy(s, r, 0).wait()
            return carry
        lax.fori_loop(0, blk, body, 0, unroll=8)

    @pl.when(jnp.logical_and(i == 0, f == 0))
    def _first_rows():
        def body(r, carry):
            row_copy(0, r, idx0_ref[r]).start()
            return carry
        lax.fori_loop(0, blk, body, 0, unroll=8)

    def prefetch_next():
        for u in range(per):
            r = f * per + u
            row_copy(nxt, r, idxn_ref[r]).start()

    @pl.when(f == 0)
    def _rows_ready():
        drain(slot)

    @pl.when(jnp.logical_and(used, f == 0))
    def _unpack():
        x_ref = xbuf_ref.at[slot]
        for c in range(ROW_SUB):
            lo, hi = _load_packed(x_ref, c, blk)
            x16_ref[:, c * LANES:(c + 1) * LANES] = lo.astype(BF16)
            x16_ref[:, HALF + c * LANES:HALF + (c + 1) * LANES] = hi.astype(BF16)

    @pl.when(used)
    def _compute():
        prefetch_next()
        x = x16_ref[...]
        hg = jnp.dot(x, wg_ref[0], preferred_element_type=F32) + bg_ref[0]
        hl = jnp.dot(x, wl_ref[0], preferred_element_type=F32) + bl_ref[0]
        glu = jnp.minimum(hg, SWIGLU_LIMIT)
        lin = jnp.clip(hl, -SWIGLU_LIMIT, SWIGLU_LIMIT)
        act = (glu * jax.nn.sigmoid(SWIGLU_ALPHA * glu) * (lin + 1.0)).astype(BF16)
        y = jnp.dot(act, w2_ref[0], preferred_element_type=F32)

        @pl.when(f == 0)
        def _first():
            acc_ref[...] = y + b2_ref[0]

        @pl.when(f > 0)
        def _rest():
            acc_ref[...] += y

        @pl.when(f == nf - 1)
        def _store():
            _store_packed(o_ref, acc_ref[...], blk)

    @pl.when(jnp.logical_not(used))
    def _unused():
        prefetch_next()

        @pl.when(f == nf - 1)
        def _zero():
            o_ref[...] = jnp.zeros(o_ref.shape, o_ref.dtype)

    @pl.when(jnp.logical_and(i == nblk - 1, f == nf - 1))
    def _last_rows():
        drain(nxt)


def _experts(hp, row_tok, blk_e, n_used, w1g, w1l, b1g, b1l, w2, b2, blk):
    rows = row_tok.shape[0]
    nblk = rows // blk
    tf = 512
    nf = D_FF // tf
    grid_spec = pltpu.PrefetchScalarGridSpec(
        num_scalar_prefetch=2,
        grid=(nblk, nf),
        in_specs=[
            pl.BlockSpec((blk,), lambda i, f, be, nb: (0,), memory_space=pltpu.SMEM),
            pl.BlockSpec((blk,), lambda i, f, be, nb: (jnp.minimum(i + 1, nblk - 1),), memory_space=pltpu.SMEM),
            pl.BlockSpec(memory_space=pl.ANY),
            pl.BlockSpec((1, D_MODEL, tf), lambda i, f, be, nb: (be[i], 0, f)),
            pl.BlockSpec((1, D_MODEL, tf), lambda i, f, be, nb: (be[i], 0, f)),
            pl.BlockSpec((1, 1, tf), lambda i, f, be, nb: (be[i], 0, f)),
            pl.BlockSpec((1, 1, tf), lambda i, f, be, nb: (be[i], 0, f)),
            pl.BlockSpec((1, tf, D_MODEL), lambda i, f, be, nb: (be[i], f, 0)),
            pl.BlockSpec((1, 1, D_MODEL), lambda i, f, be, nb: (be[i], 0, 0)),
        ],
        out_specs=pl.BlockSpec((blk * ROW_SUB, LANES), lambda i, f, be, nb: (i, 0)),
        scratch_shapes=[pltpu.VMEM((blk, D_MODEL), F32), pltpu.VMEM((blk, D_MODEL), BF16),
                        pltpu.VMEM((2, blk * ROW_SUB, LANES), jnp.uint32), pltpu.SemaphoreType.DMA((2,))],
    )
    return pl.pallas_call(
        functools.partial(_expert_kernel, nf=nf),
        grid_spec=grid_spec,
        out_shape=jax.ShapeDtypeStruct((rows * ROW_SUB, LANES), jnp.uint32),
        compiler_params=_params("arbitrary", "arbitrary"),
        name="experts",
    )(blk_e, n_used, row_tok, row_tok, hp, w1g, w1l, b1g, b1l, w2, b2)


def _combine_kernel(h_ref, y_ref, tg_ref, g_ref, b_ref, o_ref):
    tm = h_ref.shape[0]
    tg = tg_ref[...]
    los, his = [], []
    for c in range(ROW_SUB):
        lo = hi = None
        for kk in range(TOP_K):
            ylo, yhi = _load_packed(y_ref.at[kk], c, tm)
            gate = tg[:, kk:kk + 1]
            lo = gate * ylo if lo is None else lo + gate * ylo
            hi = gate * yhi if hi is None else hi + gate * yhi
        los.append(lo)
        his.append(hi)
    f = jnp.concatenate(los + his, axis=1)
    o_ref[...] = _layer_norm(DEEPNORM_ALPHA * h_ref[...] + f, g_ref[...], b_ref[...])


def _combine(h32, yg, tg, ln2_g, ln2_b, row_off):
    m = h32.shape[0]
    tm = _tile(m, 256, 16)
    off = row_off // tm
    return pl.pallas_call(
        _combine_kernel,
        grid=(m // tm,),
        in_specs=[pl.BlockSpec((tm, D_MODEL), lambda i: (i, 0)),
                  pl.BlockSpec((TOP_K, tm * ROW_SUB, LANES), lambda i: (0, i + off, 0)),
                  pl.BlockSpec((tm, LANES), lambda i: (i, 0)),
                  pl.BlockSpec((1, D_MODEL), lambda i: (0, 0)),
                  pl.BlockSpec((1, D_MODEL), lambda i: (0, 0))],
        out_specs=pl.BlockSpec((tm, D_MODEL), lambda i: (i, 0)),
        out_shape=jax.ShapeDtypeStruct((m, D_MODEL), F32),
        compiler_params=_params("parallel"),
        name="combine",
    )(h32, yg, tg, ln2_g, ln2_b)


def _route(top_e, blk):
    t = top_e.shape[0]
    n = t * TOP_K
    flat_e = top_e.reshape(-1)
    onehot = (flat_e[:, None] == jnp.arange(N_EXPERTS, dtype=jnp.int32)[None, :]).astype(jnp.int32)
    csum = jnp.cumsum(onehot, axis=0)
    rank = jnp.sum(onehot * csum, axis=1) - 1
    counts = csum[-1]
    padded = (counts + blk - 1) // blk * blk
    pad_end = jnp.cumsum(padded)
    pad_start = pad_end - padded
    start = jnp.cumsum(counts) - counts
    dest = (pad_start[flat_e] + rank).astype(jnp.int32)
    n_blocks = -(-n // blk) + N_EXPERTS
    blk_e = jnp.minimum(jnp.searchsorted(pad_end, jnp.arange(n_blocks, dtype=jnp.int32) * blk, side="right"),
                        N_EXPERTS - 1).astype(jnp.int32)
    order = jnp.argsort(flat_e, stable=True).astype(jnp.int32)
    r = jnp.arange(n_blocks * blk, dtype=jnp.int32)
    row_e = jnp.repeat(blk_e, blk)
    j = r - pad_start[row_e]
    valid = j < counts[row_e]
    row_tok = jnp.where(valid, order[jnp.clip(start[row_e] + j, 0, n - 1)] // TOP_K, 0).astype(jnp.int32)
    n_used = (pad_end[-1:] // blk).astype(jnp.int32)
    return row_tok, blk_e, n_used, dest.reshape(t, TOP_K)


def _pad_rows(a, n):
    return jnp.pad(a, ((0, n - a.shape[0]),) + ((0, 0),) * (a.ndim - 1))


def _forward(x_prompt, x_sample, cache_k, cache_v, state_hgrn, meta_tokens, ln0_g, ln0_b, w_in, lam_qk,
             subln_g, lb_logits, rnorm_g, w_pa, w_pb, w_o, ln1_g, ln1_b, ln2_g, ln2_b,
             w_router, b_router, w1, b1, w2, b2, *, expert_block, attn_tile):
    bsz, seq, _ = x_prompt.shape
    dec_b, dec_t, _ = x_sample.shape
    past = cache_k.shape[2]
    n_s = dec_b * dec_t
    n_p = bsz * seq

    row = lambda a: a.reshape(1, -1).astype(F32)
    g0, b0, g1, b1n, g2, b2n = row(ln0_g), row(ln0_b), row(ln1_g[0]), row(ln1_b[0]), row(ln2_g[0]), row(ln2_b[0])
    lb = jnp.cumsum(jax.nn.softmax(lb_logits.astype(F32), axis=0), axis=0)[0].reshape(1, -1)
    w_in16 = w_in[0].astype(BF16)
    w_pa16, w_pb16, w_o16 = w_pa[0].astype(BF16), w_pb[0].astype(BF16), w_o[0].astype(BF16)
    w_r = jnp.pad(w_router[0].astype(F32), ((0, 0), (0, LANES - N_EXPERTS)))
    b_r = jnp.pad(b_router[0].astype(F32), (0, LANES - N_EXPERTS), constant_values=-jnp.inf).reshape(1, LANES)
    w1g, w1l = _deinterleave(w1[0])
    b1g = b1[0, :, 0::2].reshape(N_EXPERTS, 1, D_FF).astype(F32)
    b1l = b1[0, :, 1::2].reshape(N_EXPERTS, 1, D_FF).astype(F32)
    w2_16 = w2[0].astype(BF16)
    b2r = b2[0].reshape(N_EXPERTS, 1, D_MODEL).astype(F32)
    lam = lam_qk[0].astype(F32)
    sub_g = subln_g[0].reshape(1, DV_A).astype(F32)
    rn_g = rnorm_g[0].reshape(1, DV_R).astype(F32)

    xs = jnp.concatenate([x_sample.reshape(n_s, D_MODEL), meta_tokens], axis=0)
    n_sm = n_s + N_META
    pos_s = jnp.concatenate([jnp.tile(N_META + past + jnp.arange(dec_t, dtype=jnp.int32), dec_b),
                             jnp.arange(N_META, dtype=jnp.int32)])
    cos_s, sin_s = _rope_tables(pos_s)
    hs16 = _ln_bf16(xs, g0, b0)
    q_s, k32_s, k16_s, v32_s, v16_s, qr_s, kr_s, lf_s, ir_s, g3_s = _in_proj(hs16, w_in16, lb, cos_s, sin_s, n_sm)
    k_meta32, v_meta32 = k32_s[n_s:], v32_s[n_s:]
    k_meta16 = _pad_rows(k16_s[n_s:], LANES)
    v_meta16 = _pad_rows(v16_s[n_s:], LANES)

    lk = past + dec_t
    lk_pad = -(-lk // LANES) * LANES
    kv_pad = ((0, 0), (0, lk_pad - lk), (0, 0))
    k_cat = jnp.pad(jnp.concatenate([cache_k[0].reshape(dec_b, past, D_MODEL).astype(BF16),
                                     k16_s[:n_s].reshape(dec_b, dec_t, D_MODEL)], axis=1), kv_pad)
    v_cat = jnp.pad(jnp.concatenate([cache_v[0].reshape(dec_b, past, D_MODEL).astype(BF16),
                                     v16_s[:n_s].reshape(dec_b, dec_t, D_MODEL)], axis=1), kv_pad)
    a_s = _diff_attn(q_s[:n_s].reshape(dec_b, dec_t, D_MODEL), k_cat, v_cat, k_meta16, v_meta16, lam, sub_g,
                     causal=False, kv_valid=lk, tq=dec_t, tk=_tile(lk_pad, 2048, LANES))

    n_str = dec_b + 1
    chunk_pad = ((0, 0), (0, CHUNK - dec_t), (0, 0))
    to_chunk = lambda a: jnp.pad(a.reshape(n_str, dec_t, a.shape[-1]), chunk_pad)
    s0 = jnp.concatenate([state_hgrn[0].astype(F32), jnp.zeros((1, H_R, DK_R, DV_R), F32)], axis=0)
    r_s, s_out = _hgrn(to_chunk(qr_s), to_chunk(kr_s), to_chunk(lf_s), to_chunk(ir_s), to_chunk(g3_s), s0, rn_g)
    s_sample, s_meta = s_out[:dec_b], s_out[dec_b:]

    mrg_s = _merge(a_s.reshape(n_s, D_MODEL), r_s[:dec_b, :dec_t].reshape(n_s, D_MODEL), w_pa16, w_pb16, g3_s[:n_s])
    h32_s, hp_s, te_s, tg_s = _post(mrg_s, w_o16, x_sample.reshape(n_s, D_MODEL), g0, b0, g1, b1n, w_r, b_r)

    xp = x_prompt.reshape(n_p, D_MODEL)
    cos_p, sin_p = _rope_tables(N_META + jnp.arange(seq, dtype=jnp.int32))
    hp16 = _ln_bf16(xp, g0, b0)
    q_p, k32_p, k16_p, v32_p, v16_p, qr_p, kr_p, lf_p, ir_p, g3_p = _in_proj(hp16, w_in16, lb, cos_p, sin_p, seq)
    t3 = lambda a: a.reshape(bsz, seq, a.shape[-1])
    ta = _tile(seq, attn_tile, CHUNK)
    a_p = _diff_attn(t3(q_p), t3(k16_p), t3(v16_p), k_meta16, v_meta16, lam, sub_g,
                     causal=True, kv_valid=seq, tq=ta, tk=ta)
    r_p, s_prompt = _hgrn(t3(qr_p), t3(kr_p), t3(lf_p), t3(ir_p), t3(g3_p),
                          jnp.broadcast_to(s_meta, (bsz, H_R, DK_R, DV_R)), rn_g)
    mrg_p = _merge(a_p.reshape(n_p, D_MODEL), r_p.reshape(n_p, D_MODEL), w_pa16, w_pb16, g3_p)
    h32_p, hp_p, te_p, tg_p = _post(mrg_p, w_o16, xp, g0, b0, g1, b1n, w_r, b_r)

    hp = jnp.concatenate([hp_p, hp_s], axis=0)
    top_e = jnp.concatenate([te_p[:, :TOP_K], te_s[:, :TOP_K]], axis=0)
    row_tok, blk_e, n_used, dest = _route(top_e, expert_block)
    yb = _experts(hp, row_tok, blk_e, n_used, w1g, w1l, b1g, b1l, w2_16, b2r, expert_block)
    yg = _gather_rows(yb, dest.T.reshape(-1)).reshape(TOP_K, (n_p + n_s) * ROW_SUB, LANES)
    y_p = _combine(h32_p, yg, tg_p, g2, b2n, 0)
    y_s = _combine(h32_s, yg, tg_s, g2, b2n, n_p)

    k_prompt = jnp.concatenate([jnp.broadcast_to(k_meta32[None], (bsz, N_META, D_MODEL)), t3(k32_p)], axis=1)
    v_prompt = jnp.concatenate([jnp.broadcast_to(v_meta32[None], (bsz, N_META, D_MODEL)), t3(v32_p)], axis=1)
    return (y_p.reshape(bsz, seq, D_MODEL),
            y_s.reshape(dec_b, dec_t, D_MODEL),
            k_prompt.reshape(1, bsz, N_META + seq, H_A, 2, DK_A),
            v_prompt.reshape(1, bsz, N_META + seq, H_A, DV_A),
            s_prompt[None],
            k32_s[:n_s].reshape(1, dec_b, dec_t, H_A, 2, DK_A),
            v32_s[:n_s].reshape(1, dec_b, dec_t, H_A, DV_A),
            s_sample[None])


def kernel(x_prompt, x_sample, cache_k, cache_v, state_hgrn, meta_tokens, ln0_g, ln0_b, w_in, lam_qk, subln_g, lb_logits, rnorm_g, w_pa, w_pb, w_o, ln1_g, ln1_b, ln2_g, ln2_b, w_router, b_router, w1, b1, w2, b2):
    return _forward(x_prompt, x_sample, cache_k, cache_v, state_hgrn, meta_tokens, ln0_g, ln0_b, w_in, lam_qk,
                    subln_g, lb_logits, rnorm_g, w_pa, w_pb, w_o, ln1_g, ln1_b, ln2_g, ln2_b,
                    w_router, b_router, w1, b1, w2, b2, expert_block=512, attn_tile=1024)
```

```python
import functools
import math

import jax
import jax.numpy as jnp
from jax import lax
from jax.experimental import pallas as pl
from jax.experimental.pallas import tpu as pltpu

F32 = jnp.float32
BF16 = jnp.bfloat16

D_MODEL = 2048
CHUNK = 64
SUB = 16
N_META = 16
H_A = 8
DK_A = 128
DV_A = 2 * DK_A
HEAD_A = 2 * DK_A
ROT_DIM = DK_A // 4
ROPE_THETA = 500000.0
H_R = 16
DK_R = D_MODEL // H_R
DV_R = D_MODEL // H_R
N_EXPERTS = 32
TOP_K = 4
D_FF = D_MODEL
SWIGLU_ALPHA = 1.702
SWIGLU_LIMIT = 7.0
LN_EPS = 1e-5
DEPTH = 1
DEEPNORM_ALPHA = (2.0 * DEPTH) ** 0.25
LAM_INIT = 0.8 - 0.6 * math.exp(-0.3 * 0)

SEG = D_MODEL
SEG_QA, SEG_KA, SEG_VA, SEG_QR, SEG_FR, SEG_IR, SEG_GR, SEG_GA, SEG_GB = range(9)

LOG2_E = math.log2(math.e)
Q_SCALE = DK_A ** -0.5 * LOG2_E
PV_PIECE = 256
HGRN_UNROLL = 8

LANES = 128
VMEM_LIMIT_BYTES = 56 * 1024 * 1024

NT_DIMS = (((1,), (1,)), ((), ()))
TN_DIMS = (((0,), (0,)), ((), ()))


def _params(*sem):
    return pltpu.CompilerParams(dimension_semantics=sem, vmem_limit_bytes=VMEM_LIMIT_BYTES)


def _tile(n, pref, mult):
    if n <= pref:
        return n
    t = pref - pref % mult
    while t > mult and n % t:
        t -= mult
    assert n % t == 0, (n, pref, mult)
    return t


def _layer_norm(x, g, b):
    mu = jnp.mean(x, axis=-1, keepdims=True)
    xc = x - mu
    var = jnp.mean(xc * xc, axis=-1, keepdims=True)
    return xc * lax.rsqrt(var + LN_EPS) * g + b


def _ln_kernel(x_ref, g_ref, b_ref, o_ref):
    o_ref[...] = _layer_norm(x_ref[...], g_ref[...], b_ref[...]).astype(o_ref.dtype)


def _ln_bf16(x, g, b):
    m = x.shape[0]
    tm = _tile(m, 512, 16)
    return pl.pallas_call(
        _ln_kernel,
        grid=(m // tm,),
        in_specs=[pl.BlockSpec((tm, D_MODEL), lambda i: (i, 0)),
                  pl.BlockSpec((1, D_MODEL), lambda i: (0, 0)),
                  pl.BlockSpec((1, D_MODEL), lambda i: (0, 0))],
        out_specs=pl.BlockSpec((tm, D_MODEL), lambda i: (i, 0)),
        out_shape=jax.ShapeDtypeStruct((m, D_MODEL), BF16),
        compiler_params=_params("parallel"),
        name="ln0",
    )(x, g, b)


def _rope_tables(pos):
    inv_freq = ROPE_THETA ** (-jnp.arange(0, ROT_DIM, 2, dtype=F32) / ROT_DIM)
    ang = pos.astype(F32)[:, None] * inv_freq[None, :]
    n = pos.shape[0]
    cos, sin = jnp.cos(ang), jnp.sin(ang)
    rest = LANES - ROT_DIM
    cos_t = jnp.concatenate([cos, cos, jnp.ones((n, rest), F32)], axis=1)
    sin_t = jnp.concatenate([-sin, sin, jnp.zeros((n, rest), F32)], axis=1)
    return cos_t, sin_t


def _proj_kernel(*refs, mode, tn):
    x_ref, w_ref = refs[:2]
    z = jnp.dot(x_ref[...], w_ref[...], preferred_element_type=F32)
    if mode == "plain":
        (o_ref,) = refs[2:]
        o_ref[...] = z.astype(o_ref.dtype)
    elif mode == "dual":
        o32_ref, o16_ref = refs[2:]
        o32_ref[...] = z
        o16_ref[...] = z.astype(BF16)
    elif mode == "gates":
        lb_ref, k_ref, lf_ref = refs[2:]
        lb = lb_ref[...]
        f = lb + (1.0 - lb) * jax.nn.sigmoid(z)
        k_ref[...] = (1.0 - f).astype(BF16)
        lf_ref[...] = jnp.log(f)
    else:
        cos_ref, sin_ref = refs[2:4]
        outs = refs[4:]
        cos = cos_ref[...]
        sin = sin_ref[...]
        lane = lax.broadcasted_iota(jnp.int32, cos.shape, 1)
        half = ROT_DIM // 2
        for c in range(tn // LANES):
            sl = slice(c * LANES, (c + 1) * LANES)
            zc = z[:, sl]
            partner = jnp.where(lane < half, pltpu.roll(zc, LANES - half, 1), pltpu.roll(zc, half, 1))
            r = zc * cos + partner * sin
            if mode == "rope_q":
                outs[0][:, sl] = (r * Q_SCALE).astype(BF16)
            else:
                outs[0][:, sl] = r
                outs[1][:, sl] = r.astype(BF16)


def _proj(hn, w, seg, nseg, mode, out_dtypes, period, extra=()):
    m = hn.shape[0]
    tm = _tile(period, 1024, 16)
    tn = 1024
    nj = nseg * SEG // tn
    off = seg * SEG // tn
    in_specs = [pl.BlockSpec((tm, D_MODEL), lambda i, j: (i, 0)),
                pl.BlockSpec((D_MODEL, tn), lambda i, j: (0, j + off))]
    if mode == "gates":
        in_specs.append(pl.BlockSpec((1, tn), lambda i, j: (0, j)))
    elif mode in ("rope_q", "rope_k"):
        nper = period // tm
        in_specs += [pl.BlockSpec((tm, LANES), lambda i, j: (i % nper, 0))] * 2
    out_specs = [pl.BlockSpec((tm, tn), lambda i, j: (i, j)) for _ in out_dtypes]
    out_shape = [jax.ShapeDtypeStruct((m, nseg * SEG), dt) for dt in out_dtypes]
    res = pl.pallas_call(
        functools.partial(_proj_kernel, mode=mode, tn=tn),
        grid=(m // tm, nj),
        in_specs=in_specs,
        out_specs=out_specs,
        out_shape=out_shape,
        compiler_params=_params("parallel", "parallel"),
        name="in_proj_" + mode,
    )(hn, w, *extra)
    return res


def _in_proj(hn, w_bf, lb, cos_t, sin_t, period):
    (q16,) = _proj(hn, w_bf, SEG_QA, 1, "rope_q", [BF16], period, (cos_t, sin_t))
    k32, k16 = _proj(hn, w_bf, SEG_KA, 1, "rope_k", [F32, BF16], period, (cos_t, sin_t))
    v32, v16 = _proj(hn, w_bf, SEG_VA, 1, "dual", [F32, BF16], period)
    (qr16,) = _proj(hn, w_bf, SEG_QR, 1, "plain", [BF16], period)
    kr16, lf32 = _proj(hn, w_bf, SEG_FR, 1, "gates", [BF16, F32], period, (lb,))
    (ir16,) = _proj(hn, w_bf, SEG_IR, 1, "plain", [BF16], period)
    (g32,) = _proj(hn, w_bf, SEG_GR, 3, "plain", [F32], period)
    return q16, k32, k16, v32, v16, qr16, kr16, lf32, ir16, g32


def _attn_kernel(qi_ref, ki_ref, q_ref, k_ref, v_ref, km_ref, vm_ref, kt_ref, vt_ref, lam_ref, g_ref, o_ref,
                 m_ref, l_ref, acc_ref, *, tq, tk, causal, tail_valid, n_kv):
    step = pl.program_id(2)
    qi = qi_ref[step]
    ki = ki_ref[step]
    row_bands = 2 if tq % 512 == 0 else 1

    def small_scores(c, keys_ref, n_valid):
        sl = slice(c * DK_A, (c + 1) * DK_A)
        s = lax.dot_general(q_ref[0, :, sl], keys_ref[0, :, sl], NT_DIMS, preferred_element_type=F32)
        col = lax.broadcasted_iota(jnp.int32, (tq, LANES), 1)
        return jnp.where(col < n_valid, s, -jnp.inf)

    @pl.when(ki == 0)
    def _meta():
        for c in range(2):
            s = small_scores(c, km_ref, N_META)
            m = jnp.broadcast_to(jnp.max(s, axis=1, keepdims=True), (tq, LANES))
            p = jnp.exp2(s - m)
            m_ref[c] = m
            l_ref[c] = p
            acc_ref[c] = jnp.dot(p.astype(BF16), vm_ref[0], preferred_element_type=F32)

    def update(c, rows, s, values, width):
        piece = PV_PIECE if width % PV_PIECE == 0 else LANES
        m_old = m_ref[c, rows, :]
        m_new = jnp.maximum(m_old, jnp.max(s, axis=1, keepdims=True))
        alpha = jnp.exp2(m_old - m_new)
        lsum = alpha * l_ref[c, rows, :]
        pv = None
        for j in range(width // piece):
            slabs = []
            for jj in range(piece // LANES):
                lo = j * piece + jj * LANES
                pj = jnp.exp2(s[:, lo:lo + LANES] - m_new)
                lsum = lsum + pj
                slabs.append(pj.astype(BF16))
            d = jnp.dot(jnp.concatenate(slabs, axis=1), values(j * piece, piece), preferred_element_type=F32)
            pv = d if pv is None else pv + d
        l_ref[c, rows, :] = lsum
        m_ref[c, rows, :] = m_new
        acc_ref[c, rows, :] = jnp.concatenate([alpha] * (DV_A // LANES), axis=1) * acc_ref[c, rows, :] + pv

    def chain(c, rows, mask):
        sl = slice(c * DK_A, (c + 1) * DK_A)
        s = lax.dot_general(q_ref[0, rows, sl], k_ref[0, :, sl].astype(BF16), NT_DIMS,
                            preferred_element_type=F32)
        if mask is not None:
            s = jnp.where(mask, s, -jnp.inf)
        update(c, rows, s, lambda lo, n: v_ref[0, lo:lo + n, :].astype(BF16), tk)

    def block(mask_fn):
        band = tq // row_bands
        for r in range(row_bands):
            for c in range(2):
                chain(c, slice(r * band, (r + 1) * band), None if mask_fn is None else mask_fn(r * band, band))

    if causal:
        @pl.when(ki != qi)
        def _full():
            block(None)

        @pl.when(ki == qi)
        def _diag():
            def chunk_mask(row0, band):
                row = lax.broadcasted_iota(jnp.int32, (band, tk), 0) + row0
                col = lax.broadcasted_iota(jnp.int32, (band, tk), 1)
                return col // CHUNK <= row // CHUNK
            block(chunk_mask)
        last = ki == qi
    else:
        block(None)
        last = ki == n_kv - 1

    @pl.when(last)
    def _finish():
        if tail_valid:
            for c in range(2):
                update(c, slice(0, tq), small_scores(c, kt_ref, tail_valid),
                       lambda lo, n: vt_ref[0, lo:lo + n, :], LANES)
        lp = lam_ref[...]
        lam = (jnp.exp(jnp.sum(lp[0:1] * lp[1:2], axis=1, keepdims=True))
               - jnp.exp(jnp.sum(lp[2:3] * lp[3:4], axis=1, keepdims=True)) + LAM_INIT)
        l0 = jnp.sum(l_ref[0], axis=1, keepdims=True)
        l1 = jnp.sum(l_ref[1], axis=1, keepdims=True)
        o = acc_ref[0] / l0 - lam * (acc_ref[1] / l1)
        ms = jnp.mean(o * o, axis=1, keepdims=True)
        o_ref[0] = (o * lax.rsqrt(ms + LN_EPS) * g_ref[...] * (1.0 - LAM_INIT)).astype(o_ref.dtype)


def _diff_attn(q, k, v, k_meta, v_meta, k_tail, v_tail, lam_qk, subln_g, *, causal, tail_valid, tq, tk):
    bsz, lq, _ = q.shape
    lk = k.shape[1]
    assert lq % tq == 0 and lk % tk == 0 and tk % LANES == 0, (lq, tq, lk, tk)
    nq, nk = lq // tq, lk // tk
    if causal:
        pairs = [(i, j) for i in range(nq) for j in range(i + 1)]
    else:
        pairs = [(i, j) for i in range(nq) for j in range(nk)]
    qi = jnp.asarray([p[0] for p in pairs], jnp.int32)
    ki = jnp.asarray([p[1] for p in pairs], jnp.int32)
    per_batch_tail = k_tail.shape[0] > 1
    tail_map = lambda b, h, s, qi, ki: (b if per_batch_tail else 0, 0, h)
    meta_map = lambda b, h, s, qi, ki: (0, 0, h)
    grid_spec = pltpu.PrefetchScalarGridSpec(
        num_scalar_prefetch=2,
        grid=(bsz, H_A, len(pairs)),
        in_specs=[
            pl.BlockSpec((1, tq, HEAD_A), lambda b, h, s, qi, ki: (b, qi[s], h)),
            pl.BlockSpec((1, tk, HEAD_A), lambda b, h, s, qi, ki: (b, ki[s], h)),
            pl.BlockSpec((1, tk, HEAD_A), lambda b, h, s, qi, ki: (b, ki[s], h)),
            pl.BlockSpec((1, LANES, HEAD_A), meta_map),
            pl.BlockSpec((1, LANES, HEAD_A), meta_map),
            pl.BlockSpec((1, LANES, HEAD_A), tail_map),
            pl.BlockSpec((1, LANES, HEAD_A), tail_map),
            pl.BlockSpec((4, DK_A), lambda b, h, s, qi, ki: (0, 0)),
            pl.BlockSpec((1, DV_A), lambda b, h, s, qi, ki: (0, 0)),
        ],
        out_specs=pl.BlockSpec((1, tq, HEAD_A), lambda b, h, s, qi, ki: (b, qi[s], h)),
        scratch_shapes=[pltpu.VMEM((2, tq, LANES), F32), pltpu.VMEM((2, tq, LANES), F32),
                        pltpu.VMEM((2, tq, DV_A), F32)],
    )
    return pl.pallas_call(
        functools.partial(_attn_kernel, tq=tq, tk=tk, causal=causal, tail_valid=tail_valid, n_kv=nk),
        grid_spec=grid_spec,
        out_shape=jax.ShapeDtypeStruct((bsz, lq, H_A * HEAD_A), BF16),
        compiler_params=_params("parallel", "parallel", "arbitrary"),
        name="diff_attn_causal" if causal else "diff_attn_cache",
    )(qi, ki, q, k, v, k_meta, v_meta, k_tail, v_tail, lam_qk, subln_g)


def _hgrn_kernel(q_ref, k_ref, lf_ref, v_ref, gr_ref, s0_ref, g_ref, o_ref, sout_ref, st_ref, *, tb):
    t = pl.program_id(2)

    @pl.when(t == 0)
    def _load_state():
        st_ref[...] = s0_ref[0, 0].T

    ones = jnp.ones((DK_R, LANES), BF16)
    row = lax.broadcasted_iota(jnp.int32, (CHUNK, DK_R), 0)
    sub = row // SUB
    tin = row % SUB
    g = g_ref[...]
    nsub = CHUNK // SUB

    def chunk(c, carry):
        sl = pl.ds(pl.multiple_of(c * CHUNK, CHUNK), CHUNK)
        qf = q_ref[0, sl, :].astype(F32)
        kf = k_ref[0, sl, :].astype(F32)
        vb = v_ref[0, sl, :]
        vf = vb.astype(F32)
        b = lf_ref[0, sl, :]
        sh = 1
        while sh < CHUNK:
            b = b + jnp.where(row >= sh, pltpu.roll(b, sh, 0), 0.0)
            sh *= 2
        b = b * LOG2_E
        b_last = b[CHUNK - 1:CHUNK, :]
        st = st_ref[...]
        qdec = (qf * jnp.exp2(b)).astype(BF16)
        o = lax.dot_general(qdec, st.astype(BF16), NT_DIMS, preferred_element_type=F32)
        ends = [b[SUB * j + SUB - 1:SUB * j + SUB, :] for j in range(nsub)]
        end_of_sub = jnp.concatenate([jnp.broadcast_to(e, (SUB, DK_R)) for e in ends], axis=0)
        kk = kf * jnp.exp2(end_of_sub - b)
        qs, ks = [], []
        for j in range(nsub - 1):
            e = jnp.exp2(jnp.where(sub > j, b - ends[j], -jnp.inf))
            qs.append((qf * e).astype(BF16))
            ks.append(jnp.where(sub == j, kk, 0.0).astype(BF16))
        sc = lax.dot_general(jnp.concatenate(qs, axis=1), jnp.concatenate(ks, axis=1), NT_DIMS,
                             preferred_element_type=F32)
        o = o + jnp.dot(sc.astype(BF16), vb, preferred_element_type=F32)
        k3 = kf.reshape(nsub, SUB, DK_R)
        b3 = b.reshape(nsub, SUB, DK_R)
        v3 = vf.reshape(nsub, SUB, DV_R)
        for s in range(SUB):
            shp = (nsub, SUB, DK_R)
            ks_b = jnp.broadcast_to(k3[:, s:s + 1, :], shp).reshape(CHUNK, DK_R)
            bs_b = jnp.broadcast_to(b3[:, s:s + 1, :], shp).reshape(CHUNK, DK_R)
            vs_b = jnp.broadcast_to(v3[:, s:s + 1, :], shp).reshape(CHUNK, DV_R)
            db = b - bs_b
            a = qf * ks_b * jnp.exp2(db if s == 0 else jnp.where(tin >= s, db, -jnp.inf))
            o = o + jnp.dot(a.astype(BF16), ones, preferred_element_type=F32) * vs_b
        kdec = (kf * jnp.exp2(b_last - b)).astype(BF16)
        st_ref[...] = jnp.exp2(b_last) * st + lax.dot_general(vb, kdec, TN_DIMS, preferred_element_type=F32)
        ms = jnp.mean(o * o, axis=1, keepdims=True)
        gr = gr_ref[0, sl, :]
        o_ref[0, sl, :] = (o * lax.rsqrt(ms + LN_EPS) * g * (gr * jax.nn.sigmoid(gr))).astype(o_ref.dtype)
        return carry

    lax.fori_loop(0, tb // CHUNK, chunk, 0, unroll=min(HGRN_UNROLL, tb // CHUNK))

    @pl.when(t == pl.num_programs(2) - 1)
    def _store_state():
        sout_ref[0, 0] = st_ref[...].T


def _hgrn(q, k, lf, v, g3, s0, rnorm_g):
    bsz, length, _ = q.shape
    tb = _tile(length, 1024, CHUNK)
    tok = lambda b, h, t: (b, t, h)
    return pl.pallas_call(
        functools.partial(_hgrn_kernel, tb=tb),
        grid=(bsz, H_R, length // tb),
        in_specs=[pl.BlockSpec((1, tb, DK_R), tok), pl.BlockSpec((1, tb, DK_R), tok),
                  pl.BlockSpec((1, tb, DK_R), tok), pl.BlockSpec((1, tb, DV_R), tok),
                  pl.BlockSpec((1, tb, DV_R), tok),
                  pl.BlockSpec((1, 1, DK_R, DV_R), lambda b, h, t: (b, h, 0, 0)),
                  pl.BlockSpec((1, DV_R), lambda b, h, t: (0, 0))],
        out_specs=[pl.BlockSpec((1, tb, DV_R), tok),
                   pl.BlockSpec((1, 1, DK_R, DV_R), lambda b, h, t: (b, h, 0, 0))],
        out_shape=[jax.ShapeDtypeStruct((bsz, length, D_MODEL), BF16),
                   jax.ShapeDtypeStruct((bsz, H_R, DK_R, DV_R), F32)],
        scratch_shapes=[pltpu.VMEM((DV_R, DK_R), F32)],
        compiler_params=_params("parallel", "parallel", "arbitrary"),
        name="hgrn2",
    )(q, k, lf, v, g3, s0, rnorm_g)


def _merge_kernel(a_ref, r_ref, wa_ref, wb_ref, ga_ref, gb_ref, o_ref):
    ya = jnp.dot(a_ref[...], wa_ref[...], preferred_element_type=F32)
    yr = jnp.dot(r_ref[...], wb_ref[...], preferred_element_type=F32)
    o_ref[...] = (jax.nn.sigmoid(ga_ref[...]) * ya + jax.nn.sigmoid(gb_ref[...]) * yr).astype(o_ref.dtype)


def _merge(a_n, r_n, w_pa, w_pb, g3):
    m = a_n.shape[0]
    tm = _tile(m, 512, 16)
    tn = 1024
    nj = D_MODEL // tn
    return pl.pallas_call(
        _merge_kernel,
        grid=(m // tm, nj),
        in_specs=[pl.BlockSpec((tm, D_MODEL), lambda i, j: (i, 0)),
                  pl.BlockSpec((tm, D_MODEL), lambda i, j: (i, 0)),
                  pl.BlockSpec((D_MODEL, tn), lambda i, j: (0, j)),
                  pl.BlockSpec((D_MODEL, tn), lambda i, j: (0, j)),
                  pl.BlockSpec((tm, tn), lambda i, j: (i, j + nj)),
                  pl.BlockSpec((tm, tn), lambda i, j: (i, j + 2 * nj))],
        out_specs=pl.BlockSpec((tm, tn), lambda i, j: (i, j)),
        out_shape=jax.ShapeDtypeStruct((m, D_MODEL), BF16),
        compiler_params=_params("parallel", "parallel"),
        name="merge",
    )(a_n, r_n, w_pa, w_pb, g3, g3)


def _post_kernel(m_ref, wo_ref, x_ref, g0_ref, b0_ref, g1_ref, b1_ref, wr_ref, br_ref,
                 h32_ref, hp_ref, te_ref, tg_ref):
    mix = jnp.dot(m_ref[...], wo_ref[...], preferred_element_type=F32)
    x = _layer_norm(x_ref[...], g0_ref[...], b0_ref[...])
    h = _layer_norm(DEEPNORM_ALPHA * x + mix, g1_ref[...], b1_ref[...])
    h32_ref[...] = h
    _store_packed(hp_ref, h, h.shape[0])
    logits = jnp.dot(h, wr_ref[...], preferred_element_type=F32, precision=lax.Precision.HIGHEST)
    logits = logits + br_ref[...]
    col = lax.broadcasted_iota(jnp.int32, logits.shape, 1)
    vals, idxs = [], []
    for _ in range(TOP_K):
        mx = jnp.max(logits, axis=1, keepdims=True)
        idx = jnp.min(jnp.where(logits == mx, col, LANES), axis=1, keepdims=True)
        vals.append(mx)
        idxs.append(idx)
        logits = jnp.where(col == idx, -jnp.inf, logits)
    es = [jnp.exp(v - vals[0]) for v in vals]
    den = es[0] + es[1] + es[2] + es[3]
    te = jnp.zeros(col.shape, jnp.int32)
    tg = jnp.zeros(col.shape, F32)
    for kk in range(TOP_K):
        te = jnp.where(col == kk, idxs[kk], te)
        tg = jnp.where(col == kk, es[kk] / den, tg)
    te_ref[...] = te
    tg_ref[...] = tg


def _post(mrg, w_o, x, ln0_g, ln0_b, ln1_g, ln1_b, w_r, b_r):
    m = mrg.shape[0]
    tm = _tile(m, 256, 16)
    row = lambda i: (i, 0)
    fix = lambda i: (0, 0)
    return pl.pallas_call(
        _post_kernel,
        grid=(m // tm,),
        in_specs=[pl.BlockSpec((tm, D_MODEL), row), pl.BlockSpec((D_MODEL, D_MODEL), fix),
                  pl.BlockSpec((tm, D_MODEL), row),
                  pl.BlockSpec((1, D_MODEL), fix), pl.BlockSpec((1, D_MODEL), fix),
                  pl.BlockSpec((1, D_MODEL), fix), pl.BlockSpec((1, D_MODEL), fix),
                  pl.BlockSpec((D_MODEL, LANES), fix), pl.BlockSpec((1, LANES), fix)],
        out_specs=[pl.BlockSpec((tm, D_MODEL), row), pl.BlockSpec((tm * ROW_SUB, LANES), row),
                   pl.BlockSpec((tm, LANES), row), pl.BlockSpec((tm, LANES), row)],
        out_shape=[jax.ShapeDtypeStruct((m, D_MODEL), F32), jax.ShapeDtypeStruct((m * ROW_SUB, LANES), jnp.uint32),
                   jax.ShapeDtypeStruct((m, LANES), jnp.int32), jax.ShapeDtypeStruct((m, LANES), F32)],
        compiler_params=_params("parallel"),
        name="post",
    )(mrg, w_o, x, ln0_g, ln0_b, ln1_g, ln1_b, w_r, b_r)


def _deint_kernel(w_ref, p_ref, g_ref, l_ref, *, tc):
    y = jnp.dot(w_ref[0].astype(BF16), p_ref[...], preferred_element_type=F32)
    g_ref[0] = y[:, :tc].astype(BF16)
    l_ref[0] = y[:, tc:].astype(BF16)


def _deinterleave(w1):
    tc, tr = 512, 1024
    r = lax.broadcasted_iota(jnp.int32, (2 * tc, 2 * tc), 0)
    c = lax.broadcasted_iota(jnp.int32, (2 * tc, 2 * tc), 1)
    perm = jnp.where(c < tc, r == 2 * c, r == 2 * (c - tc) + 1).astype(BF16)
    out = jax.ShapeDtypeStruct((N_EXPERTS, D_MODEL, D_FF), BF16)
    return pl.pallas_call(
        functools.partial(_deint_kernel, tc=tc),
        grid=(N_EXPERTS, D_MODEL // tr, D_FF // tc),
        in_specs=[pl.BlockSpec((1, tr, 2 * tc), lambda e, i, j: (e, i, j)),
                  pl.BlockSpec((2 * tc, 2 * tc), lambda e, i, j: (0, 0))],
        out_specs=[pl.BlockSpec((1, tr, tc), lambda e, i, j: (e, i, j)),
                   pl.BlockSpec((1, tr, tc), lambda e, i, j: (e, i, j))],
        out_shape=[out, out],
        compiler_params=_params("parallel", "parallel", "parallel"),
        name="w1_deinterleave",
    )(w1, perm)


ROW_SUB = 8
HALF = D_MODEL // 2


def _store_packed(o_ref, x, n):
    bits = lax.bitcast_convert_type(x.astype(BF16).astype(F32), jnp.uint32)
    for c in range(ROW_SUB):
        lo = bits[:, c * LANES:(c + 1) * LANES]
        hi = bits[:, HALF + c * LANES:HALF + (c + 1) * LANES]
        o_ref[pl.ds(c, n, stride=ROW_SUB), :] = (lo >> 16) | hi


def _load_packed(x_ref, c, n):
    w = x_ref[pl.ds(c, n, stride=ROW_SUB), :]
    lo = lax.bitcast_convert_type(w << 16, F32)
    hi = lax.bitcast_convert_type(w & jnp.uint32(0xFFFF0000), F32)
    return lo, hi


def _gather_kernel(idx_ref, src_ref, o_ref, sem, *, g):
    def row_copy(r, src_row):
        return pltpu.make_async_copy(src_ref.at[pl.ds(pl.multiple_of(src_row * ROW_SUB, ROW_SUB), ROW_SUB)],
                                     o_ref.at[pl.ds(pl.multiple_of(r * ROW_SUB, ROW_SUB), ROW_SUB)], sem)

    def issue(r, carry):
        row_copy(r, idx_ref[r]).start()
        return carry

    def drain(r, carry):
        row_copy(r, 0).wait()
        return carry

    lax.fori_loop(0, g, issue, 0, unroll=8)
    lax.fori_loop(0, g, drain, 0, unroll=8)


def _gather_rows(src, idx):
    n = idx.shape[0]
    g = _tile(n, 512, 128)
    return pl.pallas_call(
        functools.partial(_gather_kernel, g=g),
        grid=(n // g,),
        in_specs=[pl.BlockSpec((g,), lambda i: (i,), memory_space=pltpu.SMEM),
                  pl.BlockSpec(memory_space=pl.ANY)],
        out_specs=pl.BlockSpec((g * ROW_SUB, LANES), lambda i: (i, 0)),
        out_shape=jax.ShapeDtypeStruct((n * ROW_SUB, LANES), jnp.uint32),
        scratch_shapes=[pltpu.SemaphoreType.DMA(())],
        compiler_params=_params("arbitrary"),
        name="gather_rows",
    )(idx, src)


def _expert_kernel(be_ref, nb_ref, idx0_ref, idxn_ref, hp_ref, wg_ref, wl_ref, bg_ref, bl_ref, w2_ref, b2_ref,
                   o_ref, acc_ref, x16_ref, xbuf_ref, sem, *, nf):
    i = pl.program_id(0)
    f = pl.program_id(1)
    nblk = pl.num_programs(0)
    used = i < nb_ref[0]
    blk = x16_ref.shape[0]
    per = blk // nf
    slot = i % 2
    nxt = 1 - slot

    def row_copy(s, r, src_row):
        return pltpu.make_async_copy(hp_ref.at[pl.ds(pl.multiple_of(src_row * ROW_SUB, ROW_SUB), ROW_SUB)],
                                     xbuf_ref.at[s, pl.ds(pl.multiple_of(r * ROW_SUB, ROW_SUB), ROW_SUB)],
                                     sem.at[s])

    def drain(s):
        def body(r, carry):
            row_copy(s, r, 0).wait()
            return carry
        lax.fori_loop(0, blk, body, 0, unroll=8)

    @pl.when(jnp.logical_and(i == 0, f == 0))
    def _first_rows():
        def body(r, carry):
            row_copy(0, r, idx0_ref[r]).start()
            return carry
        lax.fori_loop(0, blk, body, 0, unroll=8)

    def prefetch_next():
        for u in range(per):
            r = f * per + u
            row_copy(nxt, r, idxn_ref[r]).start()

    @pl.when(f == 0)
    def _rows_ready():
        drain(slot)

    @pl.when(jnp.logical_and(used, f == 0))
    def _unpack():
        x_ref = xbuf_ref.at[slot]
        for c in range(ROW_SUB):
            lo, hi = _load_packed(x_ref, c, blk)
            x16_ref[:, c * LANES:(c + 1) * LANES] = lo.astype(BF16)
            x16_ref[:, HALF + c * LANES:HALF + (c + 1) * LANES] = hi.astype(BF16)
        acc_ref[...] = jnp.broadcast_to(b2_ref[0], acc_ref.shape)

    @pl.when(used)
    def _compute():
        prefetch_next()
        x = x16_ref[...]
        hg = jnp.dot(x, wg_ref[0], preferred_element_type=F32) + bg_ref[0]
        hl = jnp.dot(x, wl_ref[0], preferred_element_type=F32) + bl_ref[0]
        glu = jnp.minimum(hg, SWIGLU_LIMIT)
        lin = jnp.clip(hl, -SWIGLU_LIMIT, SWIGLU_LIMIT)
        act = (glu * jax.nn.sigmoid(SWIGLU_ALPHA * glu) * (lin + 1.0)).astype(BF16)
        acc_ref[...] += jnp.dot(act, w2_ref[0], preferred_element_type=F32)

        @pl.when(f == nf - 1)
        def _store():
            _store_packed(o_ref, acc_ref[...], blk)

    @pl.when(jnp.logical_not(used))
    def _unused():
        prefetch_next()

        @pl.when(f == nf - 1)
        def _zero():
            o_ref[...] = jnp.zeros(o_ref.shape, o_ref.dtype)

    @pl.when(jnp.logical_and(i == nblk - 1, f == nf - 1))
    def _last_rows():
        drain(nxt)


def _experts(hp, row_tok, blk_e, n_used, w1g, w1l, b1g, b1l, w2, b2, blk):
    rows = row_tok.shape[0]
    nblk = rows // blk
    tf = 1024
    nf = D_FF // tf
    grid_spec = pltpu.PrefetchScalarGridSpec(
        num_scalar_prefetch=2,
        grid=(nblk, nf),
        in_specs=[
            pl.BlockSpec((blk,), lambda i, f, be, nb: (0,), memory_space=pltpu.SMEM),
            pl.BlockSpec((blk,), lambda i, f, be, nb: (jnp.minimum(i + 1, nblk - 1),), memory_space=pltpu.SMEM),
            pl.BlockSpec(memory_space=pl.ANY),
            pl.BlockSpec((1, D_MODEL, tf), lambda i, f, be, nb: (be[i], 0, f)),
            pl.BlockSpec((1, D_MODEL, tf), lambda i, f, be, nb: (be[i], 0, f)),
            pl.BlockSpec((1, 1, tf), lambda i, f, be, nb: (be[i], 0, f)),
            pl.BlockSpec((1, 1, tf), lambda i, f, be, nb: (be[i], 0, f)),
            pl.BlockSpec((1, tf, D_MODEL), lambda i, f, be, nb: (be[i], f, 0)),
            pl.BlockSpec((1, 1, D_MODEL), lambda i, f, be, nb: (be[i], 0, 0)),
        ],
        out_specs=pl.BlockSpec((blk * ROW_SUB, LANES), lambda i, f, be, nb: (i, 0)),
        scratch_shapes=[pltpu.VMEM((blk, D_MODEL), F32), pltpu.VMEM((blk, D_MODEL), BF16),
                        pltpu.VMEM((2, blk * ROW_SUB, LANES), jnp.uint32), pltpu.SemaphoreType.DMA((2,))],
    )
    return pl.pallas_call(
        functools.partial(_expert_kernel, nf=nf),
        grid_spec=grid_spec,
        out_shape=jax.ShapeDtypeStruct((rows * ROW_SUB, LANES), jnp.uint32),
        compiler_params=_params("arbitrary", "arbitrary"),
        name="experts",
    )(blk_e, n_used, row_tok, row_tok, hp, w1g, w1l, b1g, b1l, w2, b2)


def _combine_kernel(h_ref, y_ref, tg_ref, g_ref, b_ref, o_ref):
    tm = h_ref.shape[0]
    tg = tg_ref[...]
    los, his = [], []
    for c in range(ROW_SUB):
        lo = hi = None
        for kk in range(TOP_K):
            ylo, yhi = _load_packed(y_ref.at[kk], c, tm)
            gate = tg[:, kk:kk + 1]
            lo = gate * ylo if lo is None else lo + gate * ylo
            hi = gate * yhi if hi is None else hi + gate * yhi
        los.append(lo)
        his.append(hi)
    f = jnp.concatenate(los + his, axis=1)
    o_ref[...] = _layer_norm(DEEPNORM_ALPHA * h_ref[...] + f, g_ref[...], b_ref[...])


def _combine(h32, yg, tg, ln2_g, ln2_b, row_off):
    m = h32.shape[0]
    tm = _tile(m, 256, 16)
    off = row_off // tm
    return pl.pallas_call(
        _combine_kernel,
        grid=(m // tm,),
        in_specs=[pl.BlockSpec((tm, D_MODEL), lambda i: (i, 0)),
                  pl.BlockSpec((TOP_K, tm * ROW_SUB, LANES), lambda i: (0, i + off, 0)),
                  pl.BlockSpec((tm, LANES), lambda i: (i, 0)),
                  pl.BlockSpec((1, D_MODEL), lambda i: (0, 0)),
                  pl.BlockSpec((1, D_MODEL), lambda i: (0, 0))],
        out_specs=pl.BlockSpec((tm, D_MODEL), lambda i: (i, 0)),
        out_shape=jax.ShapeDtypeStruct((m, D_MODEL), F32),
        compiler_params=_params("parallel"),
        name="combine",
    )(h32, yg, tg, ln2_g, ln2_b)


def _route(top_e, blk):
    t = top_e.shape[0]
    n = t * TOP_K
    flat_e = top_e.reshape(-1)
    onehot = (flat_e[:, None] == jnp.arange(N_EXPERTS, dtype=jnp.int32)[None, :]).astype(jnp.int32)
    csum = jnp.cumsum(onehot, axis=0)
    rank = jnp.sum(onehot * csum, axis=1) - 1
    counts = csum[-1]
    padded = (counts + blk - 1) // blk * blk
    pad_end = jnp.cumsum(padded)
    pad_start = pad_end - padded
    start = jnp.cumsum(counts) - counts
    dest = (pad_start[flat_e] + rank).astype(jnp.int32)
    n_blocks = -(-n // blk) + N_EXPERTS
    blk_e = jnp.minimum(jnp.searchsorted(pad_end, jnp.arange(n_blocks, dtype=jnp.int32) * blk, side="right"),
                        N_EXPERTS - 1).astype(jnp.int32)
    order = jnp.argsort(flat_e, stable=True).astype(jnp.int32)
    r = jnp.arange(n_blocks * blk, dtype=jnp.int32)
    row_e = jnp.repeat(blk_e, blk)
    j = r - pad_start[row_e]
    valid = j < counts[row_e]
    row_tok = jnp.where(valid, order[jnp.clip(start[row_e] + j, 0, n - 1)] // TOP_K, 0).astype(jnp.int32)
    n_used = (pad_end[-1:] // blk).astype(jnp.int32)
    return row_tok, blk_e, n_used, dest.reshape(t, TOP_K)


def _pad_rows(a, n):
    return jnp.pad(a, ((0, n - a.shape[0]),) + ((0, 0),) * (a.ndim - 1))


def _forward(x_prompt, x_sample, cache_k, cache_v, state_hgrn, meta_tokens, ln0_g, ln0_b, w_in, lam_qk,
             subln_g, lb_logits, rnorm_g, w_pa, w_pb, w_o, ln1_g, ln1_b, ln2_g, ln2_b,
             w_router, b_router, w1, b1, w2, b2, *, expert_block, attn_tile):
    bsz, seq, _ = x_prompt.shape
    dec_b, dec_t, _ = x_sample.shape
    past = cache_k.shape[2]
    n_s = dec_b * dec_t
    n_p = bsz * seq

    row = lambda a: a.reshape(1, -1).astype(F32)
    g0, b0, g1, b1n, g2, b2n = row(ln0_g), row(ln0_b), row(ln1_g[0]), row(ln1_b[0]), row(ln2_g[0]), row(ln2_b[0])
    lb = jnp.cumsum(jax.nn.softmax(lb_logits.astype(F32), axis=0), axis=0)[0].reshape(1, -1)
    w_in16 = w_in[0].astype(BF16)
    w_pa16, w_pb16, w_o16 = w_pa[0].astype(BF16), w_pb[0].astype(BF16), w_o[0].astype(BF16)
    w_r = jnp.pad(w_router[0].astype(F32), ((0, 0), (0, LANES - N_EXPERTS)))
    b_r = jnp.pad(b_router[0].astype(F32), (0, LANES - N_EXPERTS), constant_values=-jnp.inf).reshape(1, LANES)
    w1g, w1l = _deinterleave(w1[0])
    b1g = b1[0, :, 0::2].reshape(N_EXPERTS, 1, D_FF).astype(F32)
    b1l = b1[0, :, 1::2].reshape(N_EXPERTS, 1, D_FF).astype(F32)
    w2_16 = w2[0].astype(BF16)
    b2r = b2[0].reshape(N_EXPERTS, 1, D_MODEL).astype(F32)
    lam = lam_qk[0].astype(F32)
    sub_g = subln_g[0].reshape(1, DV_A).astype(F32)
    rn_g = rnorm_g[0].reshape(1, DV_R).astype(F32)

    xs = jnp.concatenate([x_sample.reshape(n_s, D_MODEL), meta_tokens], axis=0)
    n_sm = n_s + N_META
    pos_s = jnp.concatenate([jnp.tile(N_META + past + jnp.arange(dec_t, dtype=jnp.int32), dec_b),
                             jnp.arange(N_META, dtype=jnp.int32)])
    cos_s, sin_s = _rope_tables(pos_s)
    hs16 = _ln_bf16(xs, g0, b0)
    q_s, k32_s, k16_s, v32_s, v16_s, qr_s, kr_s, lf_s, ir_s, g3_s = _in_proj(hs16, w_in16, lb, cos_s, sin_s, n_sm)
    k_meta32, v_meta32 = k32_s[n_s:], v32_s[n_s:]
    k_meta16 = _pad_rows(k16_s[n_s:], LANES)[None]
    v_meta16 = _pad_rows(v16_s[n_s:], LANES)[None]

    tail_pad = ((0, 0), (0, LANES - dec_t), (0, 0))
    k_new = jnp.pad(k16_s[:n_s].reshape(dec_b, dec_t, D_MODEL), tail_pad)
    v_new = jnp.pad(v16_s[:n_s].reshape(dec_b, dec_t, D_MODEL), tail_pad)
    a_s = _diff_attn(q_s[:n_s].reshape(dec_b, dec_t, D_MODEL),
                     cache_k[0].reshape(dec_b, past, D_MODEL), cache_v[0].reshape(dec_b, past, D_MODEL),
                     k_meta16, v_meta16, k_new, v_new, lam, sub_g,
                     causal=False, tail_valid=dec_t, tq=dec_t, tk=_tile(past, 1024, LANES))

    n_str = dec_b + 1
    chunk_pad = ((0, 0), (0, CHUNK - dec_t), (0, 0))
    to_chunk = lambda a: jnp.pad(a.reshape(n_str, dec_t, a.shape[-1]), chunk_pad)
    s0 = jnp.concatenate([state_hgrn[0].astype(F32), jnp.zeros((1, H_R, DK_R, DV_R), F32)], axis=0)
    r_s, s_out = _hgrn(to_chunk(qr_s), to_chunk(kr_s), to_chunk(lf_s), to_chunk(ir_s), to_chunk(g3_s), s0, rn_g)
    s_sample, s_meta = s_out[:dec_b], s_out[dec_b:]

    mrg_s = _merge(a_s.reshape(n_s, D_MODEL), r_s[:dec_b, :dec_t].reshape(n_s, D_MODEL), w_pa16, w_pb16, g3_s[:n_s])
    h32_s, hp_s, te_s, tg_s = _post(mrg_s, w_o16, x_sample.reshape(n_s, D_MODEL), g0, b0, g1, b1n, w_r, b_r)

    xp = x_prompt.reshape(n_p, D_MODEL)
    cos_p, sin_p = _rope_tables(N_META + jnp.arange(seq, dtype=jnp.int32))
    hp16 = _ln_bf16(xp, g0, b0)
    q_p, k32_p, k16_p, v32_p, v16_p, qr_p, kr_p, lf_p, ir_p, g3_p = _in_proj(hp16, w_in16, lb, cos_p, sin_p, seq)
    t3 = lambda a: a.reshape(bsz, seq, a.shape[-1])
    ta = _tile(seq, attn_tile, CHUNK)
    a_p = _diff_attn(t3(q_p), t3(k16_p), t3(v16_p), k_meta16, v_meta16, k_meta16, v_meta16, lam, sub_g,
                     causal=True, tail_valid=0, tq=ta, tk=ta)
    r_p, s_prompt = _hgrn(t3(qr_p), t3(kr_p), t3(lf_p), t3(ir_p), t3(g3_p),
                          jnp.broadcast_to(s_meta, (bsz, H_R, DK_R, DV_R)), rn_g)
    mrg_p = _merge(a_p.reshape(n_p, D_MODEL), r_p.reshape(n_p, D_MODEL), w_pa16, w_pb16, g3_p)
    h32_p, hp_p, te_p, tg_p = _post(mrg_p, w_o16, xp, g0, b0, g1, b1n, w_r, b_r)

    hp = jnp.concatenate([hp_p, hp_s], axis=0)
    top_e = jnp.concatenate([te_p[:, :TOP_K], te_s[:, :TOP_K]], axis=0)
    row_tok, blk_e, n_used, dest = _route(top_e, expert_block)
    yb = _experts(hp, row_tok, blk_e, n_used, w1g, w1l, b1g, b1l, w2_16, b2r, expert_block)
    yg = _gather_rows(yb, dest.T.reshape(-1)).reshape(TOP_K, (n_p + n_s) * ROW_SUB, LANES)
    y_p = _combine(h32_p, yg, tg_p, g2, b2n, 0)
    y_s = _combine(h32_s, yg, tg_s, g2, b2n, n_p)

    k_prompt = jnp.concatenate([jnp.broadcast_to(k_meta32[None], (bsz, N_META, D_MODEL)), t3(k32_p)], axis=1)
    v_prompt = jnp.concatenate([jnp.broadcast_to(v_meta32[None], (bsz, N_META, D_MODEL)), t3(v32_p)], axis=1)
    return (y_p.reshape(bsz, seq, D_MODEL),
            y_s.reshape(dec_b, dec_t, D_MODEL),
            k_prompt.reshape(1, bsz, N_META + seq, H_A, 2, DK_A),
            v_prompt.reshape(1, bsz, N_META + seq, H_A, DV_A),
            s_prompt[None],
            k32_s[:n_s].reshape(1, dec_b, dec_t, H_A, 2, DK_A),
            v32_s[:n_s].reshape(1, dec_b, dec_t, H_A, DV_A),
            s_sample[None])


def kernel(x_prompt, x_sample, cache_k, cache_v, state_hgrn, meta_tokens, ln0_g, ln0_b, w_in, lam_qk, subln_g, lb_logits, rnorm_g, w_pa, w_pb, w_o, ln1_g, ln1_b, ln2_g, ln2_b, w_router, b_router, w1, b1, w2, b2):
    return _forward(x_prompt, x_sample, cache_k, cache_v, state_hgrn, meta_tokens, ln0_g, ln0_b, w_in, lam_qk,
                    subln_g, lb_logits, rnorm_g, w_pa, w_pb, w_o, ln1_g, ln1_b, ln2_g, ln2_b,
                    w_router, b_router, w1, b1, w2, b2, expert_block=512, attn_tile=1024)
```

```python
import functools
import math

import jax
import jax.numpy as jnp
from jax import lax
from jax.experimental import pallas as pl
from jax.experimental.pallas import tpu as pltpu

F32 = jnp.float32
BF16 = jnp.bfloat16

D_MODEL = 2048
CHUNK = 64
SUB = 16
N_META = 16
H_A = 8
DK_A = 128
DV_A = 2 * DK_A
HEAD_A = 2 * DK_A
ROT_DIM = DK_A // 4
ROPE_THETA = 500000.0
H_R = 16
DK_R = D_MODEL // H_R
DV_R = D_MODEL // H_R
N_EXPERTS = 32
TOP_K = 4
D_FF = D_MODEL
SWIGLU_ALPHA = 1.702
SWIGLU_LIMIT = 7.0
LN_EPS = 1e-5
DEPTH = 1
DEEPNORM_ALPHA = (2.0 * DEPTH) ** 0.25
LAM_INIT = 0.8 - 0.6 * math.exp(-0.3 * 0)

SEG = D_MODEL
SEG_QA, SEG_KA, SEG_VA, SEG_QR, SEG_FR, SEG_IR, SEG_GR, SEG_GA, SEG_GB = range(9)

LOG2_E = math.log2(math.e)
Q_SCALE = DK_A ** -0.5 * LOG2_E
PV_PIECE = 256
HGRN_UNROLL = 8

LANES = 128
VMEM_LIMIT_BYTES = 56 * 1024 * 1024

NT_DIMS = (((1,), (1,)), ((), ()))
TN_DIMS = (((0,), (0,)), ((), ()))


def _params(*sem):
    return pltpu.CompilerParams(dimension_semantics=sem, vmem_limit_bytes=VMEM_LIMIT_BYTES)


def _tile(n, pref, mult):
    if n <= pref:
        return n
    t = pref - pref % mult
    while t > mult and n % t:
        t -= mult
    assert n % t == 0, (n, pref, mult)
    return t


def _layer_norm(x, g, b):
    mu = jnp.mean(x, axis=-1, keepdims=True)
    xc = x - mu
    var = jnp.mean(xc * xc, axis=-1, keepdims=True)
    return xc * lax.rsqrt(var + LN_EPS) * g + b


def _ln_kernel(x_ref, g_ref, b_ref, o_ref):
    o_ref[...] = _layer_norm(x_ref[...], g_ref[...], b_ref[...]).astype(o_ref.dtype)


def _ln_bf16(x, g, b):
    m = x.shape[0]
    tm = _tile(m, 512, 16)
    return pl.pallas_call(
        _ln_kernel,
        grid=(m // tm,),
        in_specs=[pl.BlockSpec((tm, D_MODEL), lambda i: (i, 0)),
                  pl.BlockSpec((1, D_MODEL), lambda i: (0, 0)),
                  pl.BlockSpec((1, D_MODEL), lambda i: (0, 0))],
        out_specs=pl.BlockSpec((tm, D_MODEL), lambda i: (i, 0)),
        out_shape=jax.ShapeDtypeStruct((m, D_MODEL), BF16),
        compiler_params=_params("parallel"),
        name="ln0",
    )(x, g, b)


def _rope_tables(pos):
    inv_freq = ROPE_THETA ** (-jnp.arange(0, ROT_DIM, 2, dtype=F32) / ROT_DIM)
    ang = pos.astype(F32)[:, None] * inv_freq[None, :]
    n = pos.shape[0]
    cos, sin = jnp.cos(ang), jnp.sin(ang)
    rest = LANES - ROT_DIM
    cos_t = jnp.concatenate([cos, cos, jnp.ones((n, rest), F32)], axis=1)
    sin_t = jnp.concatenate([-sin, sin, jnp.zeros((n, rest), F32)], axis=1)
    return cos_t, sin_t


def _proj_kernel(*refs, mode, tn):
    x_ref, w_ref = refs[:2]
    z = jnp.dot(x_ref[...], w_ref[...], preferred_element_type=F32)
    if mode == "plain":
        (o_ref,) = refs[2:]
        o_ref[...] = z.astype(o_ref.dtype)
    elif mode == "dual":
        o32_ref, o16_ref = refs[2:]
        o32_ref[...] = z
        o16_ref[...] = z.astype(BF16)
    elif mode == "gates":
        lb_ref, k_ref, lf_ref = refs[2:]
        lb = lb_ref[...]
        f = lb + (1.0 - lb) * jax.nn.sigmoid(z)
        k_ref[...] = (1.0 - f).astype(BF16)
        lf_ref[...] = jnp.log(f)
    else:
        cos_ref, sin_ref = refs[2:4]
        outs = refs[4:]
        cos = cos_ref[...]
        sin = sin_ref[...]
        lane = lax.broadcasted_iota(jnp.int32, cos.shape, 1)
        half = ROT_DIM // 2
        for c in range(tn // LANES):
            sl = slice(c * LANES, (c + 1) * LANES)
            zc = z[:, sl]
            partner = jnp.where(lane < half, pltpu.roll(zc, LANES - half, 1), pltpu.roll(zc, half, 1))
            r = zc * cos + partner * sin
            if mode == "rope_q":
                outs[0][:, sl] = (r * Q_SCALE).astype(BF16)
            else:
                outs[0][:, sl] = r
                outs[1][:, sl] = r.astype(BF16)


def _proj(hn, w, seg, nseg, mode, out_dtypes, period, extra=()):
    m = hn.shape[0]
    tm = _tile(period, 1024, 16)
    tn = 1024
    nj = nseg * SEG // tn
    off = seg * SEG // tn
    in_specs = [pl.BlockSpec((tm, D_MODEL), lambda i, j: (i, 0)),
                pl.BlockSpec((D_MODEL, tn), lambda i, j: (0, j + off))]
    if mode == "gates":
        in_specs.append(pl.BlockSpec((1, tn), lambda i, j: (0, j)))
    elif mode in ("rope_q", "rope_k"):
        nper = period // tm
        in_specs += [pl.BlockSpec((tm, LANES), lambda i, j: (i % nper, 0))] * 2
    out_specs = [pl.BlockSpec((tm, tn), lambda i, j: (i, j)) for _ in out_dtypes]
    out_shape = [jax.ShapeDtypeStruct((m, nseg * SEG), dt) for dt in out_dtypes]
    res = pl.pallas_call(
        functools.partial(_proj_kernel, mode=mode, tn=tn),
        grid=(m // tm, nj),
        in_specs=in_specs,
        out_specs=out_specs,
        out_shape=out_shape,
        compiler_params=_params("parallel", "parallel"),
        name="in_proj_" + mode,
    )(hn, w, *extra)
    return res


def _in_proj(hn, w_bf, lb, cos_t, sin_t, period):
    (q16,) = _proj(hn, w_bf, SEG_QA, 1, "rope_q", [BF16], period, (cos_t, sin_t))
    k32, k16 = _proj(hn, w_bf, SEG_KA, 1, "rope_k", [F32, BF16], period, (cos_t, sin_t))
    v32, v16 = _proj(hn, w_bf, SEG_VA, 1, "dual", [F32, BF16], period)
    (qr16,) = _proj(hn, w_bf, SEG_QR, 1, "plain", [BF16], period)
    kr16, lf32 = _proj(hn, w_bf, SEG_FR, 1, "gates", [BF16, F32], period, (lb,))
    (ir16,) = _proj(hn, w_bf, SEG_IR, 1, "plain", [BF16], period)
    (g32,) = _proj(hn, w_bf, SEG_GR, 3, "plain", [F32], period)
    return q16, k32, k16, v32, v16, qr16, kr16, lf32, ir16, g32


def _attn_kernel(qi_ref, ki_ref, q_ref, k_ref, v_ref, km_ref, vm_ref, kt_ref, vt_ref, lam_ref, g_ref, o_ref,
                 m_ref, l_ref, acc_ref, *, tq, tk, causal, tail_valid, n_kv):
    step = pl.program_id(2)
    qi = qi_ref[step]
    ki = ki_ref[step]
    row_bands = 2 if tq % 512 == 0 else 1

    def small_scores(c, keys_ref, n_valid):
        sl = slice(c * DK_A, (c + 1) * DK_A)
        s = lax.dot_general(q_ref[0, :, sl], keys_ref[0, :, sl], NT_DIMS, preferred_element_type=F32)
        col = lax.broadcasted_iota(jnp.int32, (tq, LANES), 1)
        return jnp.where(col < n_valid, s, -jnp.inf)

    @pl.when(ki == 0)
    def _meta():
        for c in range(2):
            s = small_scores(c, km_ref, N_META)
            m = jnp.broadcast_to(jnp.max(s, axis=1, keepdims=True), (tq, LANES))
            p = jnp.exp2(s - m)
            m_ref[c] = m
            l_ref[c] = p
            acc_ref[c] = jnp.dot(p.astype(BF16), vm_ref[0], preferred_element_type=F32)

    def update(c, rows, s, values, width):
        piece = PV_PIECE if width % PV_PIECE == 0 else LANES
        m_old = m_ref[c, rows, :]
        m_new = jnp.maximum(m_old, jnp.max(s, axis=1, keepdims=True))
        alpha = jnp.exp2(m_old - m_new)
        lsum = alpha * l_ref[c, rows, :]
        pv = None
        for j in range(width // piece):
            slabs = []
            for jj in range(piece // LANES):
                lo = j * piece + jj * LANES
                pj = jnp.exp2(s[:, lo:lo + LANES] - m_new)
                lsum = lsum + pj
                slabs.append(pj.astype(BF16))
            d = jnp.dot(jnp.concatenate(slabs, axis=1), values(j * piece, piece), preferred_element_type=F32)
            pv = d if pv is None else pv + d
        l_ref[c, rows, :] = lsum
        m_ref[c, rows, :] = m_new
        acc_ref[c, rows, :] = jnp.concatenate([alpha] * (DV_A // LANES), axis=1) * acc_ref[c, rows, :] + pv

    def chain(c, rows, mask):
        sl = slice(c * DK_A, (c + 1) * DK_A)
        s = lax.dot_general(q_ref[0, rows, sl], k_ref[0, :, sl].astype(BF16), NT_DIMS,
                            preferred_element_type=F32)
        return s if mask is None else jnp.where(mask, s, -jnp.inf)

    def block(mask_fn):
        band = tq // row_bands
        chains = [(c, slice(r * band, (r + 1) * band), None if mask_fn is None else mask_fn(r * band, band))
                  for r in range(row_bands) for c in range(2)]
        scores = [chain(*ch) for ch in chains]
        for (c, rows, _), s in zip(chains, scores):
            update(c, rows, s, lambda lo, n: v_ref[0, lo:lo + n, :].astype(BF16), tk)

    if causal:
        @pl.when(ki != qi)
        def _full():
            block(None)

        @pl.when(ki == qi)
        def _diag():
            def chunk_mask(row0, band):
                row = lax.broadcasted_iota(jnp.int32, (band, tk), 0) + row0
                col = lax.broadcasted_iota(jnp.int32, (band, tk), 1)
                return col // CHUNK <= row // CHUNK
            block(chunk_mask)
        last = ki == qi
    else:
        block(None)
        last = ki == n_kv - 1

    @pl.when(last)
    def _finish():
        if tail_valid:
            for c in range(2):
                update(c, slice(0, tq), small_scores(c, kt_ref, tail_valid),
                       lambda lo, n: vt_ref[0, lo:lo + n, :], LANES)
        lp = lam_ref[...]
        lam = (jnp.exp(jnp.sum(lp[0:1] * lp[1:2], axis=1, keepdims=True))
               - jnp.exp(jnp.sum(lp[2:3] * lp[3:4], axis=1, keepdims=True)) + LAM_INIT)
        l0 = jnp.sum(l_ref[0], axis=1, keepdims=True)
        l1 = jnp.sum(l_ref[1], axis=1, keepdims=True)
        o = acc_ref[0] / l0 - lam * (acc_ref[1] / l1)
        ms = jnp.mean(o * o, axis=1, keepdims=True)
        o_ref[0] = (o * lax.rsqrt(ms + LN_EPS) * g_ref[...] * (1.0 - LAM_INIT)).astype(o_ref.dtype)


def _diff_attn(q, k, v, k_meta, v_meta, k_tail, v_tail, lam_qk, subln_g, *, causal, tail_valid, tq, tk):
    bsz, lq, _ = q.shape
    lk = k.shape[1]
    assert lq % tq == 0 and lk % tk == 0 and tk % LANES == 0, (lq, tq, lk, tk)
    nq, nk = lq // tq, lk // tk
    if causal:
        pairs = [(i, j) for i in range(nq) for j in range(i + 1)]
    else:
        pairs = [(i, j) for i in range(nq) for j in range(nk)]
    qi = jnp.asarray([p[0] for p in pairs], jnp.int32)
    ki = jnp.asarray([p[1] for p in pairs], jnp.int32)
    per_batch_tail = k_tail.shape[0] > 1
    tail_map = lambda b, h, s, qi, ki: (b if per_batch_tail else 0, 0, h)
    meta_map = lambda b, h, s, qi, ki: (0, 0, h)
    grid_spec = pltpu.PrefetchScalarGridSpec(
        num_scalar_prefetch=2,
        grid=(bsz, H_A, len(pairs)),
        in_specs=[
            pl.BlockSpec((1, tq, HEAD_A), lambda b, h, s, qi, ki: (b, qi[s], h)),
            pl.BlockSpec((1, tk, HEAD_A), lambda b, h, s, qi, ki: (b, ki[s], h)),
            pl.BlockSpec((1, tk, HEAD_A), lambda b, h, s, qi, ki: (b, ki[s], h)),
            pl.BlockSpec((1, LANES, HEAD_A), meta_map),
            pl.BlockSpec((1, LANES, HEAD_A), meta_map),
            pl.BlockSpec((1, LANES, HEAD_A), tail_map),
            pl.BlockSpec((1, LANES, HEAD_A), tail_map),
            pl.BlockSpec((4, DK_A), lambda b, h, s, qi, ki: (0, 0)),
            pl.BlockSpec((1, DV_A), lambda b, h, s, qi, ki: (0, 0)),
        ],
        out_specs=pl.BlockSpec((1, tq, HEAD_A), lambda b, h, s, qi, ki: (b, qi[s], h)),
        scratch_shapes=[pltpu.VMEM((2, tq, LANES), F32), pltpu.VMEM((2, tq, LANES), F32),
                        pltpu.VMEM((2, tq, DV_A), F32)],
    )
    return pl.pallas_call(
        functools.partial(_attn_kernel, tq=tq, tk=tk, causal=causal, tail_valid=tail_valid, n_kv=nk),
        grid_spec=grid_spec,
        out_shape=jax.ShapeDtypeStruct((bsz, lq, H_A * HEAD_A), BF16),
        compiler_params=_params("parallel", "parallel", "arbitrary"),
        name="diff_attn_causal" if causal else "diff_attn_cache",
    )(qi, ki, q, k, v, k_meta, v_meta, k_tail, v_tail, lam_qk, subln_g)


def _hgrn_kernel(q_ref, k_ref, lf_ref, v_ref, gr_ref, s0_ref, g_ref, o_ref, sout_ref, st_ref, *, tb):
    t = pl.program_id(2)

    @pl.when(t == 0)
    def _load_state():
        st_ref[...] = s0_ref[0, 0].T

    ones = jnp.ones((DK_R, LANES), BF16)
    row = lax.broadcasted_iota(jnp.int32, (CHUNK, DK_R), 0)
    sub = row // SUB
    tin = row % SUB
    g = g_ref[...]
    nsub = CHUNK // SUB

    def chunk(c, carry):
        sl = pl.ds(pl.multiple_of(c * CHUNK, CHUNK), CHUNK)
        qf = q_ref[0, sl, :].astype(F32)
        kf = k_ref[0, sl, :].astype(F32)
        vb = v_ref[0, sl, :]
        vf = vb.astype(F32)
        b = lf_ref[0, sl, :]
        sh = 1
        while sh < CHUNK:
            b = b + jnp.where(row >= sh, pltpu.roll(b, sh, 0), 0.0)
            sh *= 2
        b = b * LOG2_E
        b_last = b[CHUNK - 1:CHUNK, :]
        st = st_ref[...]
        qdec = (qf * jnp.exp2(b)).astype(BF16)
        o = lax.dot_general(qdec, st.astype(BF16), NT_DIMS, preferred_element_type=F32)
        ends = [b[SUB * j + SUB - 1:SUB * j + SUB, :] for j in range(nsub)]
        end_of_sub = jnp.concatenate([jnp.broadcast_to(e, (SUB, DK_R)) for e in ends], axis=0)
        kk = kf * jnp.exp2(end_of_sub - b)
        qs, ks = [], []
        for j in range(nsub - 1):
            e = jnp.exp2(jnp.where(sub > j, b - ends[j], -jnp.inf))
            qs.append((qf * e).astype(BF16))
            ks.append(jnp.where(sub == j, kk, 0.0).astype(BF16))
        sc = lax.dot_general(jnp.concatenate(qs, axis=1), jnp.concatenate(ks, axis=1), NT_DIMS,
                             preferred_element_type=F32)
        o = o + jnp.dot(sc.astype(BF16), vb, preferred_element_type=F32)
        k3 = kf.reshape(nsub, SUB, DK_R)
        b3 = b.reshape(nsub, SUB, DK_R)
        v3 = vf.reshape(nsub, SUB, DV_R)
        for s in range(SUB):
            shp = (nsub, SUB, DK_R)
            ks_b = jnp.broadcast_to(k3[:, s:s + 1, :], shp).reshape(CHUNK, DK_R)
            bs_b = jnp.broadcast_to(b3[:, s:s + 1, :], shp).reshape(CHUNK, DK_R)
            vs_b = jnp.broadcast_to(v3[:, s:s + 1, :], shp).reshape(CHUNK, DV_R)
            db = b - bs_b
            a = qf * ks_b * jnp.exp2(db if s == 0 else jnp.where(tin >= s, db, -jnp.inf))
            o = o + jnp.dot(a.astype(BF16), ones, preferred_element_type=F32) * vs_b
        kdec = (kf * jnp.exp2(b_last - b)).astype(BF16)
        st_ref[...] = jnp.exp2(b_last) * st + lax.dot_general(vb, kdec, TN_DIMS, preferred_element_type=F32)
        ms = jnp.mean(o * o, axis=1, keepdims=True)
        gr = gr_ref[0, sl, :]
        o_ref[0, sl, :] = (o * lax.rsqrt(ms + LN_EPS) * g * (gr * jax.nn.sigmoid(gr))).astype(o_ref.dtype)
        return carry

    lax.fori_loop(0, tb // CHUNK, chunk, 0, unroll=min(HGRN_UNROLL, tb // CHUNK))

    @pl.when(t == pl.num_programs(2) - 1)
    def _store_state():
        sout_ref[0, 0] = st_ref[...].T


def _hgrn(q, k, lf, v, g3, s0, rnorm_g):
    bsz, length, _ = q.shape
    tb = _tile(length, 1024, CHUNK)
    tok = lambda b, h, t: (b, t, h)
    return pl.pallas_call(
        functools.partial(_hgrn_kernel, tb=tb),
        grid=(bsz, H_R, length // tb),
        in_specs=[pl.BlockSpec((1, tb, DK_R), tok), pl.BlockSpec((1, tb, DK_R), tok),
                  pl.BlockSpec((1, tb, DK_R), tok), pl.BlockSpec((1, tb, DV_R), tok),
                  pl.BlockSpec((1, tb, DV_R), tok),
                  pl.BlockSpec((1, 1, DK_R, DV_R), lambda b, h, t: (b, h, 0, 0)),
                  pl.BlockSpec((1, DV_R), lambda b, h, t: (0, 0))],
        out_specs=[pl.BlockSpec((1, tb, DV_R), tok),
                   pl.BlockSpec((1, 1, DK_R, DV_R), lambda b, h, t: (b, h, 0, 0))],
        out_shape=[jax.ShapeDtypeStruct((bsz, length, D_MODEL), BF16),
                   jax.ShapeDtypeStruct((bsz, H_R, DK_R, DV_R), F32)],
        scratch_shapes=[pltpu.VMEM((DV_R, DK_R), F32)],
        compiler_params=_params("parallel", "parallel", "arbitrary"),
        name="hgrn2",
    )(q, k, lf, v, g3, s0, rnorm_g)


def _merge_kernel(a_ref, r_ref, wa_ref, wb_ref, ga_ref, gb_ref, o_ref):
    ya = jnp.dot(a_ref[...], wa_ref[...], preferred_element_type=F32)
    yr = jnp.dot(r_ref[...], wb_ref[...], preferred_element_type=F32)
    o_ref[...] = (jax.nn.sigmoid(ga_ref[...]) * ya + jax.nn.sigmoid(gb_ref[...]) * yr).astype(o_ref.dtype)


def _merge(a_n, r_n, w_pa, w_pb, g3):
    m = a_n.shape[0]
    tm = _tile(m, 512, 16)
    tn = 1024
    nj = D_MODEL // tn
    return pl.pallas_call(
        _merge_kernel,
        grid=(m // tm, nj),
        in_specs=[pl.BlockSpec((tm, D_MODEL), lambda i, j: (i, 0)),
                  pl.BlockSpec((tm, D_MODEL), lambda i, j: (i, 0)),
                  pl.BlockSpec((D_MODEL, tn), lambda i, j: (0, j)),
                  pl.BlockSpec((D_MODEL, tn), lambda i, j: (0, j)),
                  pl.BlockSpec((tm, tn), lambda i, j: (i, j + nj)),
                  pl.BlockSpec((tm, tn), lambda i, j: (i, j + 2 * nj))],
        out_specs=pl.BlockSpec((tm, tn), lambda i, j: (i, j)),
        out_shape=jax.ShapeDtypeStruct((m, D_MODEL), BF16),
        compiler_params=_params("parallel", "parallel"),
        name="merge",
    )(a_n, r_n, w_pa, w_pb, g3, g3)


def _post_kernel(m_ref, wo_ref, x_ref, g0_ref, b0_ref, g1_ref, b1_ref, wr_ref, br_ref,
                 h32_ref, hp_ref, te_ref, tg_ref, *, band):
    for r0 in range(0, m_ref.shape[0], band):
        rows = slice(r0, r0 + band)
        mix = jnp.dot(m_ref[rows, :], wo_ref[...], preferred_element_type=F32)
        x = _layer_norm(x_ref[rows, :], g0_ref[...], b0_ref[...])
        h = _layer_norm(DEEPNORM_ALPHA * x + mix, g1_ref[...], b1_ref[...])
        h32_ref[rows, :] = h
        _store_packed(hp_ref.at[pl.ds(r0 * ROW_SUB, band * ROW_SUB)], h, band)
        logits = jnp.dot(h, wr_ref[...], preferred_element_type=F32, precision=lax.Precision.HIGHEST)
        logits = logits + br_ref[...]
        col = lax.broadcasted_iota(jnp.int32, logits.shape, 1)
        vals, idxs = [], []
        for _ in range(TOP_K):
            mx = jnp.max(logits, axis=1, keepdims=True)
            idx = jnp.min(jnp.where(logits == mx, col, LANES), axis=1, keepdims=True)
            vals.append(mx)
            idxs.append(idx)
            logits = jnp.where(col == idx, -jnp.inf, logits)
        es = [jnp.exp(v - vals[0]) for v in vals]
        den = es[0] + es[1] + es[2] + es[3]
        te = jnp.zeros(col.shape, jnp.int32)
        tg = jnp.zeros(col.shape, F32)
        for kk in range(TOP_K):
            te = jnp.where(col == kk, idxs[kk], te)
            tg = jnp.where(col == kk, es[kk] / den, tg)
        te_ref[rows, :] = te
        tg_ref[rows, :] = tg


def _post(mrg, w_o, x, ln0_g, ln0_b, ln1_g, ln1_b, w_r, b_r):
    m = mrg.shape[0]
    tm = _tile(m, 512, 16)
    band = _tile(tm, 256, 16)
    row = lambda i: (i, 0)
    fix = lambda i: (0, 0)
    once = dict(pipeline_mode=pl.Buffered(1))
    return pl.pallas_call(
        functools.partial(_post_kernel, band=band),
        grid=(m // tm,),
        in_specs=[pl.BlockSpec((tm, D_MODEL), row), pl.BlockSpec((D_MODEL, D_MODEL), fix, **once),
                  pl.BlockSpec((tm, D_MODEL), row),
                  pl.BlockSpec((1, D_MODEL), fix), pl.BlockSpec((1, D_MODEL), fix),
                  pl.BlockSpec((1, D_MODEL), fix), pl.BlockSpec((1, D_MODEL), fix),
                  pl.BlockSpec((D_MODEL, LANES), fix, **once), pl.BlockSpec((1, LANES), fix)],
        out_specs=[pl.BlockSpec((tm, D_MODEL), row), pl.BlockSpec((tm * ROW_SUB, LANES), row),
                   pl.BlockSpec((tm, LANES), row), pl.BlockSpec((tm, LANES), row)],
        out_shape=[jax.ShapeDtypeStruct((m, D_MODEL), F32), jax.ShapeDtypeStruct((m * ROW_SUB, LANES), jnp.uint32),
                   jax.ShapeDtypeStruct((m, LANES), jnp.int32), jax.ShapeDtypeStruct((m, LANES), F32)],
        compiler_params=_params("parallel"),
        name="post",
    )(mrg, w_o, x, ln0_g, ln0_b, ln1_g, ln1_b, w_r, b_r)


def _deint_kernel(w_ref, p_ref, g_ref, l_ref, *, tc):
    y = jnp.dot(w_ref[0].astype(BF16), p_ref[...], preferred_element_type=F32)
    g_ref[0] = y[:, :tc].astype(BF16)
    l_ref[0] = y[:, tc:].astype(BF16)


def _deinterleave(w1):
    tc, tr = 512, 1024
    r = lax.broadcasted_iota(jnp.int32, (2 * tc, 2 * tc), 0)
    c = lax.broadcasted_iota(jnp.int32, (2 * tc, 2 * tc), 1)
    perm = jnp.where(c < tc, r == 2 * c, r == 2 * (c - tc) + 1).astype(BF16)
    out = jax.ShapeDtypeStruct((N_EXPERTS, D_MODEL, D_FF), BF16)
    return pl.pallas_call(
        functools.partial(_deint_kernel, tc=tc),
        grid=(N_EXPERTS, D_MODEL // tr, D_FF // tc),
        in_specs=[pl.BlockSpec((1, tr, 2 * tc), lambda e, i, j: (e, i, j)),
                  pl.BlockSpec((2 * tc, 2 * tc), lambda e, i, j: (0, 0))],
        out_specs=[pl.BlockSpec((1, tr, tc), lambda e, i, j: (e, i, j)),
                   pl.BlockSpec((1, tr, tc), lambda e, i, j: (e, i, j))],
        out_shape=[out, out],
        compiler_params=_params("parallel", "parallel", "parallel"),
        name="w1_deinterleave",
    )(w1, perm)


ROW_SUB = 8
HALF = D_MODEL // 2


def _store_packed(o_ref, x, n):
    bits = lax.bitcast_convert_type(x.astype(BF16).astype(F32), jnp.uint32)
    for c in range(ROW_SUB):
        lo = bits[:, c * LANES:(c + 1) * LANES]
        hi = bits[:, HALF + c * LANES:HALF + (c + 1) * LANES]
        o_ref[pl.ds(c, n, stride=ROW_SUB), :] = (lo >> 16) | hi


def _load_packed(x_ref, c, n):
    w = x_ref[pl.ds(c, n, stride=ROW_SUB), :]
    lo = lax.bitcast_convert_type(w << 16, F32)
    hi = lax.bitcast_convert_type(w & jnp.uint32(0xFFFF0000), F32)
    return lo, hi


def _gather_kernel(idx_ref, src_ref, o_ref, sem, *, g):
    def row_copy(r, src_row):
        return pltpu.make_async_copy(src_ref.at[pl.ds(pl.multiple_of(src_row * ROW_SUB, ROW_SUB), ROW_SUB)],
                                     o_ref.at[pl.ds(pl.multiple_of(r * ROW_SUB, ROW_SUB), ROW_SUB)], sem)

    def issue(r, carry):
        row_copy(r, idx_ref[r]).start()
        return carry

    def drain(r, carry):
        row_copy(r, 0).wait()
        return carry

    lax.fori_loop(0, g, issue, 0, unroll=8)
    lax.fori_loop(0, g, drain, 0, unroll=8)


def _gather_rows(src, idx):
    n = idx.shape[0]
    g = _tile(n, 512, 128)
    return pl.pallas_call(
        functools.partial(_gather_kernel, g=g),
        grid=(n // g,),
        in_specs=[pl.BlockSpec((g,), lambda i: (i,), memory_space=pltpu.SMEM),
                  pl.BlockSpec(memory_space=pl.ANY)],
        out_specs=pl.BlockSpec((g * ROW_SUB, LANES), lambda i: (i, 0)),
        out_shape=jax.ShapeDtypeStruct((n * ROW_SUB, LANES), jnp.uint32),
        scratch_shapes=[pltpu.SemaphoreType.DMA(())],
        compiler_params=_params("arbitrary"),
        name="gather_rows",
    )(idx, src)


def _expert_kernel(be_ref, nb_ref, idx0_ref, idxn_ref, hp_ref, wg_ref, wl_ref, bg_ref, bl_ref, w2_ref, b2_ref,
                   o_ref, acc_ref, x16_ref, xbuf_ref, sem, *, nf):
    i = pl.program_id(0)
    f = pl.program_id(1)
    nblk = pl.num_programs(0)
    used = i < nb_ref[0]
    blk = x16_ref.shape[0]
    per = blk // nf
    slot = i % 2
    nxt = 1 - slot

    def row_copy(s, r, src_row):
        return pltpu.make_async_copy(hp_ref.at[pl.ds(pl.multiple_of(src_row * ROW_SUB, ROW_SUB), ROW_SUB)],
                                     xbuf_ref.at[s, pl.ds(pl.multiple_of(r * ROW_SUB, ROW_SUB), ROW_SUB)],
                                     sem.at[s])

    def drain(s):
        def body(r, carry):
            row_copy(s, r, 0).wait()
            return carry
        lax.fori_loop(0, blk, body, 0, unroll=8)

    @pl.when(jnp.logical_and(i == 0, f == 0))
    def _first_rows():
        def body(r, carry):
            row_copy(0, r, idx0_ref[r]).start()
            return carry
        lax.fori_loop(0, blk, body, 0, unroll=8)

    def prefetch_next():
        for u in range(per):
            r = f * per + u
            row_copy(nxt, r, idxn_ref[r]).start()

    @pl.when(f == 0)
    def _rows_ready():
        drain(slot)

    @pl.when(jnp.logical_and(used, f == 0))
    def _unpack():
        x_ref = xbuf_ref.at[slot]
        for c in range(ROW_SUB):
            lo, hi = _load_packed(x_ref, c, blk)
            x16_ref[:, c * LANES:(c + 1) * LANES] = lo.astype(BF16)
            x16_ref[:, HALF + c * LANES:HALF + (c + 1) * LANES] = hi.astype(BF16)
        acc_ref[...] = jnp.broadcast_to(b2_ref[0], acc_ref.shape)

    @pl.when(used)
    def _compute():
        prefetch_next()
        x = x16_ref[...]
        hg = jnp.dot(x, wg_ref[0], preferred_element_type=F32) + bg_ref[0]
        hl = jnp.dot(x, wl_ref[0], preferred_element_type=F32) + bl_ref[0]
        glu = jnp.minimum(hg, SWIGLU_LIMIT)
        lin = jnp.clip(hl, -SWIGLU_LIMIT, SWIGLU_LIMIT)
        act = (glu * jax.nn.sigmoid(SWIGLU_ALPHA * glu) * (lin + 1.0)).astype(BF16)
        acc_ref[...] += jnp.dot(act, w2_ref[0], preferred_element_type=F32)

        @pl.when(f == nf - 1)
        def _store():
            _store_packed(o_ref, acc_ref[...], blk)

    @pl.when(jnp.logical_not(used))
    def _unused():
        prefetch_next()

        @pl.when(f == nf - 1)
        def _zero():
            o_ref[...] = jnp.zeros(o_ref.shape, o_ref.dtype)

    @pl.when(jnp.logical_and(i == nblk - 1, f == nf - 1))
    def _last_rows():
        drain(nxt)


def _experts(hp, row_tok, blk_e, n_used, w1g, w1l, b1g, b1l, w2, b2, blk):
    rows = row_tok.shape[0]
    nblk = rows // blk
    tf = 1024
    nf = D_FF // tf
    grid_spec = pltpu.PrefetchScalarGridSpec(
        num_scalar_prefetch=2,
        grid=(nblk, nf),
        in_specs=[
            pl.BlockSpec((blk,), lambda i, f, be, nb: (0,), memory_space=pltpu.SMEM),
            pl.BlockSpec((blk,), lambda i, f, be, nb: (jnp.minimum(i + 1, nblk - 1),), memory_space=pltpu.SMEM),
            pl.BlockSpec(memory_space=pl.ANY),
            pl.BlockSpec((1, D_MODEL, tf), lambda i, f, be, nb: (be[i], 0, f)),
            pl.BlockSpec((1, D_MODEL, tf), lambda i, f, be, nb: (be[i], 0, f)),
            pl.BlockSpec((1, 1, tf), lambda i, f, be, nb: (be[i], 0, f)),
            pl.BlockSpec((1, 1, tf), lambda i, f, be, nb: (be[i], 0, f)),
            pl.BlockSpec((1, tf, D_MODEL), lambda i, f, be, nb: (be[i], f, 0)),
            pl.BlockSpec((1, 1, D_MODEL), lambda i, f, be, nb: (be[i], 0, 0)),
        ],
        out_specs=pl.BlockSpec((blk * ROW_SUB, LANES), lambda i, f, be, nb: (i, 0)),
        scratch_shapes=[pltpu.VMEM((blk, D_MODEL), F32), pltpu.VMEM((blk, D_MODEL), BF16),
                        pltpu.VMEM((2, blk * ROW_SUB, LANES), jnp.uint32), pltpu.SemaphoreType.DMA((2,))],
    )
    return pl.pallas_call(
        functools.partial(_expert_kernel, nf=nf),
        grid_spec=grid_spec,
        out_shape=jax.ShapeDtypeStruct((rows * ROW_SUB, LANES), jnp.uint32),
        compiler_params=_params("arbitrary", "arbitrary"),
        name="experts",
    )(blk_e, n_used, row_tok, row_tok, hp, w1g, w1l, b1g, b1l, w2, b2)


def _combine_kernel(h_ref, y_ref, tg_ref, g_ref, b_ref, o_ref):
    tm = h_ref.shape[0]
    tg = tg_ref[...]
    los, his = [], []
    for c in range(ROW_SUB):
        lo = hi = None
        for kk in range(TOP_K):
            ylo, yhi = _load_packed(y_ref.at[kk], c, tm)
            gate = tg[:, kk:kk + 1]
            lo = gate * ylo if lo is None else lo + gate * ylo
            hi = gate * yhi if hi is None else hi + gate * yhi
        los.append(lo)
        his.append(hi)
    f = jnp.concatenate(los + his, axis=1)
    o_ref[...] = _layer_norm(DEEPNORM_ALPHA * h_ref[...] + f, g_ref[...], b_ref[...])


def _combine(h32, yg, tg, ln2_g, ln2_b, row_off):
    m = h32.shape[0]
    tm = _tile(m, 256, 16)
    off = row_off // tm
    return pl.pallas_call(
        _combine_kernel,
        grid=(m // tm,),
        in_specs=[pl.BlockSpec((tm, D_MODEL), lambda i: (i, 0)),
                  pl.BlockSpec((TOP_K, tm * ROW_SUB, LANES), lambda i: (0, i + off, 0)),
                  pl.BlockSpec((tm, LANES), lambda i: (i, 0)),
                  pl.BlockSpec((1, D_MODEL), lambda i: (0, 0)),
                  pl.BlockSpec((1, D_MODEL), lambda i: (0, 0))],
        out_specs=pl.BlockSpec((tm, D_MODEL), lambda i: (i, 0)),
        out_shape=jax.ShapeDtypeStruct((m, D_MODEL), F32),
        compiler_params=_params("parallel"),
        name="combine",
    )(h32, yg, tg, ln2_g, ln2_b)


def _route(top_e, blk):
    t = top_e.shape[0]
    n = t * TOP_K
    flat_e = top_e.reshape(-1)
    onehot = (flat_e[:, None] == jnp.arange(N_EXPERTS, dtype=jnp.int32)[None, :]).astype(jnp.int32)
    csum = jnp.cumsum(onehot, axis=0)
    rank = jnp.sum(onehot * csum, axis=1) - 1
    counts = csum[-1]
    padded = (counts + blk - 1) // blk * blk
    pad_end = jnp.cumsum(padded)
    pad_start = pad_end - padded
    start = jnp.cumsum(counts) - counts
    dest = (pad_start[flat_e] + rank).astype(jnp.int32)
    n_blocks = -(-n // blk) + N_EXPERTS
    blk_e = jnp.minimum(jnp.searchsorted(pad_end, jnp.arange(n_blocks, dtype=jnp.int32) * blk, side="right"),
                        N_EXPERTS - 1).astype(jnp.int32)
    order = jnp.argsort(flat_e, stable=True).astype(jnp.int32)
    r = jnp.arange(n_blocks * blk, dtype=jnp.int32)
    row_e = jnp.repeat(blk_e, blk)
    j = r - pad_start[row_e]
    valid = j < counts[row_e]
    row_tok = jnp.where(valid, order[jnp.clip(start[row_e] + j, 0, n - 1)] // TOP_K, 0).astype(jnp.int32)
    n_used = (pad_end[-1:] // blk).astype(jnp.int32)
    return row_tok, blk_e, n_used, dest.reshape(t, TOP_K)


def _pad_rows(a, n):
    return jnp.pad(a, ((0, n - a.shape[0]),) + ((0, 0),) * (a.ndim - 1))


def _forward(x_prompt, x_sample, cache_k, cache_v, state_hgrn, meta_tokens, ln0_g, ln0_b, w_in, lam_qk,
             subln_g, lb_logits, rnorm_g, w_pa, w_pb, w_o, ln1_g, ln1_b, ln2_g, ln2_b,
             w_router, b_router, w1, b1, w2, b2, *, expert_block, attn_tile):
    bsz, seq, _ = x_prompt.shape
    dec_b, dec_t, _ = x_sample.shape
    past = cache_k.shape[2]
    n_s = dec_b * dec_t
    n_p = bsz * seq

    row = lambda a: a.reshape(1, -1).astype(F32)
    g0, b0, g1, b1n, g2, b2n = row(ln0_g), row(ln0_b), row(ln1_g[0]), row(ln1_b[0]), row(ln2_g[0]), row(ln2_b[0])
    lb = jnp.cumsum(jax.nn.softmax(lb_logits.astype(F32), axis=0), axis=0)[0].reshape(1, -1)
    w_in16 = w_in[0].astype(BF16)
    w_pa16, w_pb16, w_o16 = w_pa[0].astype(BF16), w_pb[0].astype(BF16), w_o[0].astype(BF16)
    w_r = jnp.pad(w_router[0].astype(F32), ((0, 0), (0, LANES - N_EXPERTS)))
    b_r = jnp.pad(b_router[0].astype(F32), (0, LANES - N_EXPERTS), constant_values=-jnp.inf).reshape(1, LANES)
    w1g, w1l = _deinterleave(w1[0])
    b1g = b1[0, :, 0::2].reshape(N_EXPERTS, 1, D_FF).astype(F32)
    b1l = b1[0, :, 1::2].reshape(N_EXPERTS, 1, D_FF).astype(F32)
    w2_16 = w2[0].astype(BF16)
    b2r = b2[0].reshape(N_EXPERTS, 1, D_MODEL).astype(F32)
    lam = lam_qk[0].astype(F32)
    sub_g = subln_g[0].reshape(1, DV_A).astype(F32)
    rn_g = rnorm_g[0].reshape(1, DV_R).astype(F32)

    xs = jnp.concatenate([x_sample.reshape(n_s, D_MODEL), meta_tokens], axis=0)
    n_sm = n_s + N_META
    pos_s = jnp.concatenate([jnp.tile(N_META + past + jnp.arange(dec_t, dtype=jnp.int32), dec_b),
                             jnp.arange(N_META, dtype=jnp.int32)])
    cos_s, sin_s = _rope_tables(pos_s)
    hs16 = _ln_bf16(xs, g0, b0)
    q_s, k32_s, k16_s, v32_s, v16_s, qr_s, kr_s, lf_s, ir_s, g3_s = _in_proj(hs16, w_in16, lb, cos_s, sin_s, n_sm)
    k_meta32, v_meta32 = k32_s[n_s:], v32_s[n_s:]
    k_meta16 = _pad_rows(k16_s[n_s:], LANES)[None]
    v_meta16 = _pad_rows(v16_s[n_s:], LANES)[None]

    tail_pad = ((0, 0), (0, LANES - dec_t), (0, 0))
    k_new = jnp.pad(k16_s[:n_s].reshape(dec_b, dec_t, D_MODEL), tail_pad)
    v_new = jnp.pad(v16_s[:n_s].reshape(dec_b, dec_t, D_MODEL), tail_pad)
    a_s = _diff_attn(q_s[:n_s].reshape(dec_b, dec_t, D_MODEL),
                     cache_k[0].reshape(dec_b, past, D_MODEL), cache_v[0].reshape(dec_b, past, D_MODEL),
                     k_meta16, v_meta16, k_new, v_new, lam, sub_g,
                     causal=False, tail_valid=dec_t, tq=dec_t, tk=_tile(past, 1024, LANES))

    n_str = dec_b + 1
    chunk_pad = ((0, 0), (0, CHUNK - dec_t), (0, 0))
    to_chunk = lambda a: jnp.pad(a.reshape(n_str, dec_t, a.shape[-1]), chunk_pad)
    s0 = jnp.concatenate([state_hgrn[0].astype(F32), jnp.zeros((1, H_R, DK_R, DV_R), F32)], axis=0)
    r_s, s_out = _hgrn(to_chunk(qr_s), to_chunk(kr_s), to_chunk(lf_s), to_chunk(ir_s), to_chunk(g3_s), s0, rn_g)
    s_sample, s_meta = s_out[:dec_b], s_out[dec_b:]

    mrg_s = _merge(a_s.reshape(n_s, D_MODEL), r_s[:dec_b, :dec_t].reshape(n_s, D_MODEL), w_pa16, w_pb16, g3_s[:n_s])
    h32_s, hp_s, te_s, tg_s = _post(mrg_s, w_o16, x_sample.reshape(n_s, D_MODEL), g0, b0, g1, b1n, w_r, b_r)

    xp = x_prompt.reshape(n_p, D_MODEL)
    cos_p, sin_p = _rope_tables(N_META + jnp.arange(seq, dtype=jnp.int32))
    hp16 = _ln_bf16(xp, g0, b0)
    q_p, k32_p, k16_p, v32_p, v16_p, qr_p, kr_p, lf_p, ir_p, g3_p = _in_proj(hp16, w_in16, lb, cos_p, sin_p, seq)
    t3 = lambda a: a.reshape(bsz, seq, a.shape[-1])
    ta = _tile(seq, attn_tile, CHUNK)
    a_p = _diff_attn(t3(q_p), t3(k16_p), t3(v16_p), k_meta16, v_meta16, k_meta16, v_meta16, lam, sub_g,
                     causal=True, tail_valid=0, tq=ta, tk=ta)
    r_p, s_prompt = _hgrn(t3(qr_p), t3(kr_p), t3(lf_p), t3(ir_p), t3(g3_p),
                          jnp.broadcast_to(s_meta, (bsz, H_R, DK_R, DV_R)), rn_g)
    mrg_p = _merge(a_p.reshape(n_p, D_MODEL), r_p.reshape(n_p, D_MODEL), w_pa16, w_pb16, g3_p)
    h32_p, hp_p, te_p, tg_p = _post(mrg_p, w_o16, xp, g0, b0, g1, b1n, w_r, b_r)

    hp = jnp.concatenate([hp_p, hp_s], axis=0)
    top_e = jnp.concatenate([te_p[:, :TOP_K], te_s[:, :TOP_K]], axis=0)
    row_tok, blk_e, n_used, dest = _route(top_e, expert_block)
    yb = _experts(hp, row_tok, blk_e, n_used, w1g, w1l, b1g, b1l, w2_16, b2r, expert_block)
    yg = _gather_rows(yb, dest.T.reshape(-1)).reshape(TOP_K, (n_p + n_s) * ROW_SUB, LANES)
    y_p = _combine(h32_p, yg, tg_p, g2, b2n, 0)
    y_s = _combine(h32_s, yg, tg_s, g2, b2n, n_p)

    k_prompt = jnp.concatenate([jnp.broadcast_to(k_meta32[None], (bsz, N_META, D_MODEL)), t3(k32_p)], axis=1)
    v_prompt = jnp.concatenate([jnp.broadcast_to(v_meta32[None], (bsz, N_META, D_MODEL)), t3(v32_p)], axis=1)
    return (y_p.reshape(bsz, seq, D_MODEL),
            y_s.reshape(dec_b, dec_t, D_MODEL),
            k_prompt.reshape(1, bsz, N_META + seq, H_A, 2, DK_A),
            v_prompt.reshape(1, bsz, N_META + seq, H_A, DV_A),
            s_prompt[None],
            k32_s[:n_s].reshape(1, dec_b, dec_t, H_A, 2, DK_A),
            v32_s[:n_s].reshape(1, dec_b, dec_t, H_A, DV_A),
            s_sample[None])


def kernel(x_prompt, x_sample, cache_k, cache_v, state_hgrn, meta_tokens, ln0_g, ln0_b, w_in, lam_qk, subln_g, lb_logits, rnorm_g, w_pa, w_pb, w_o, ln1_g, ln1_b, ln2_g, ln2_b, w_router, b_router, w1, b1, w2, b2):
    return _forward(x_prompt, x_sample, cache_k, cache_v, state_hgrn, meta_tokens, ln0_g, ln0_b, w_in, lam_qk,
                    subln_g, lb_logits, rnorm_g, w_pa, w_pb, w_o, ln1_g, ln1_b, ln2_g, ln2_b,
                    w_router, b_router, w1, b1, w2, b2, expert_block=512, attn_tile=1024)
```

```python
import functools
import math

import jax
import jax.numpy as jnp
from jax import lax
from jax.experimental import pallas as pl
from jax.experimental.pallas import tpu as pltpu

F32 = jnp.float32
BF16 = jnp.bfloat16

D_MODEL = 2048
CHUNK = 64
SUB = 16
N_META = 16
H_A = 8
DK_A = 128
DV_A = 2 * DK_A
HEAD_A = 2 * DK_A
ROT_DIM = DK_A // 4
ROPE_THETA = 500000.0
H_R = 16
DK_R = D_MODEL // H_R
DV_R = D_MODEL // H_R
N_EXPERTS = 32
TOP_K = 4
D_FF = D_MODEL
SWIGLU_ALPHA = 1.702
SWIGLU_LIMIT = 7.0
LN_EPS = 1e-5
DEPTH = 1
DEEPNORM_ALPHA = (2.0 * DEPTH) ** 0.25
LAM_INIT = 0.8 - 0.6 * math.exp(-0.3 * 0)

SEG = D_MODEL
SEG_QA, SEG_KA, SEG_VA, SEG_QR, SEG_FR, SEG_IR, SEG_GR, SEG_GA, SEG_GB = range(9)

LOG2_E = math.log2(math.e)
Q_SCALE = DK_A ** -0.5 * LOG2_E
PV_PIECE = 256
HGRN_UNROLL = 16

LANES = 128
VMEM_LIMIT_BYTES = 56 * 1024 * 1024

NT_DIMS = (((1,), (1,)), ((), ()))
TN_DIMS = (((0,), (0,)), ((), ()))


def _params(*sem):
    return pltpu.CompilerParams(dimension_semantics=sem, vmem_limit_bytes=VMEM_LIMIT_BYTES)


def _tile(n, pref, mult):
    if n <= pref:
        return n
    t = pref - pref % mult
    while t > mult and n % t:
        t -= mult
    assert n % t == 0, (n, pref, mult)
    return t


def _layer_norm(x, g, b):
    mu = jnp.mean(x, axis=-1, keepdims=True)
    xc = x - mu
    var = jnp.mean(xc * xc, axis=-1, keepdims=True)
    return xc * lax.rsqrt(var + LN_EPS) * g + b


def _ln_kernel(x_ref, g_ref, b_ref, o_ref):
    o_ref[...] = _layer_norm(x_ref[...], g_ref[...], b_ref[...]).astype(o_ref.dtype)


def _ln_bf16(x, g, b):
    m = x.shape[0]
    tm = _tile(m, 512, 16)
    return pl.pallas_call(
        _ln_kernel,
        grid=(m // tm,),
        in_specs=[pl.BlockSpec((tm, D_MODEL), lambda i: (i, 0)),
                  pl.BlockSpec((1, D_MODEL), lambda i: (0, 0)),
                  pl.BlockSpec((1, D_MODEL), lambda i: (0, 0))],
        out_specs=pl.BlockSpec((tm, D_MODEL), lambda i: (i, 0)),
        out_shape=jax.ShapeDtypeStruct((m, D_MODEL), BF16),
        compiler_params=_params("parallel"),
        name="ln0",
    )(x, g, b)


def _rope_tables(pos):
    inv_freq = ROPE_THETA ** (-jnp.arange(0, ROT_DIM, 2, dtype=F32) / ROT_DIM)
    ang = pos.astype(F32)[:, None] * inv_freq[None, :]
    n = pos.shape[0]
    cos, sin = jnp.cos(ang), jnp.sin(ang)
    rest = LANES - ROT_DIM
    cos_t = jnp.concatenate([cos, cos, jnp.ones((n, rest), F32)], axis=1)
    sin_t = jnp.concatenate([-sin, sin, jnp.zeros((n, rest), F32)], axis=1)
    return cos_t, sin_t


def _proj_kernel(*refs, mode, tn):
    x_ref, w_ref = refs[:2]
    z = jnp.dot(x_ref[...], w_ref[...], preferred_element_type=F32)
    if mode == "plain":
        (o_ref,) = refs[2:]
        o_ref[...] = z.astype(o_ref.dtype)
    elif mode == "dual":
        o32_ref, o16_ref = refs[2:]
        o32_ref[...] = z
        o16_ref[...] = z.astype(BF16)
    elif mode == "gates":
        lb_ref, k_ref, lf_ref = refs[2:]
        lb = lb_ref[...]
        f = lb + (1.0 - lb) * jax.nn.sigmoid(z)
        k_ref[...] = (1.0 - f).astype(BF16)
        lf_ref[...] = jnp.log(f)
    else:
        cos_ref, sin_ref = refs[2:4]
        outs = refs[4:]
        cos = cos_ref[...]
        sin = sin_ref[...]
        lane = lax.broadcasted_iota(jnp.int32, cos.shape, 1)
        half = ROT_DIM // 2
        for c in range(tn // LANES):
            sl = slice(c * LANES, (c + 1) * LANES)
            zc = z[:, sl]
            partner = jnp.where(lane < half, pltpu.roll(zc, LANES - half, 1), pltpu.roll(zc, half, 1))
            r = zc * cos + partner * sin
            if mode == "rope_q":
                outs[0][:, sl] = (r * Q_SCALE).astype(BF16)
            else:
                outs[0][:, sl] = r
                outs[1][:, sl] = r.astype(BF16)


def _proj(hn, w, seg, nseg, mode, out_dtypes, period, extra=()):
    m = hn.shape[0]
    tm = _tile(period, 1024, 16)
    tn = 1024
    nj = nseg * SEG // tn
    off = seg * SEG // tn
    in_specs = [pl.BlockSpec((tm, D_MODEL), lambda i, j: (i, 0)),
                pl.BlockSpec((D_MODEL, tn), lambda i, j: (0, j + off))]
    if mode == "gates":
        in_specs.append(pl.BlockSpec((1, tn), lambda i, j: (0, j)))
    elif mode in ("rope_q", "rope_k"):
        nper = period // tm
        in_specs += [pl.BlockSpec((tm, LANES), lambda i, j: (i % nper, 0))] * 2
    out_specs = [pl.BlockSpec((tm, tn), lambda i, j: (i, j)) for _ in out_dtypes]
    out_shape = [jax.ShapeDtypeStruct((m, nseg * SEG), dt) for dt in out_dtypes]
    res = pl.pallas_call(
        functools.partial(_proj_kernel, mode=mode, tn=tn),
        grid=(m // tm, nj),
        in_specs=in_specs,
        out_specs=out_specs,
        out_shape=out_shape,
        compiler_params=_params("parallel", "parallel"),
        name="in_proj_" + mode,
    )(hn, w, *extra)
    return res


def _in_proj(hn, w_bf, lb, cos_t, sin_t, period):
    (q16,) = _proj(hn, w_bf, SEG_QA, 1, "rope_q", [BF16], period, (cos_t, sin_t))
    k32, k16 = _proj(hn, w_bf, SEG_KA, 1, "rope_k", [F32, BF16], period, (cos_t, sin_t))
    v32, v16 = _proj(hn, w_bf, SEG_VA, 1, "dual", [F32, BF16], period)
    (qr16,) = _proj(hn, w_bf, SEG_QR, 1, "plain", [BF16], period)
    kr16, lf32 = _proj(hn, w_bf, SEG_FR, 1, "gates", [BF16, F32], period, (lb,))
    (ir16,) = _proj(hn, w_bf, SEG_IR, 1, "plain", [BF16], period)
    (g32,) = _proj(hn, w_bf, SEG_GR, 3, "plain", [F32], period)
    return q16, k32, k16, v32, v16, qr16, kr16, lf32, ir16, g32


def _attn_kernel(qi_ref, ki_ref, q_ref, k_ref, v_ref, km_ref, vm_ref, kt_ref, vt_ref, lam_ref, g_ref, o_ref,
                 m_ref, l_ref, acc_ref, *, tq, tk, causal, tail_valid, n_kv):
    step = pl.program_id(2)
    qi = qi_ref[step]
    ki = ki_ref[step]
    row_bands = 2 if tq % 512 == 0 else 1

    def small_scores(c, keys_ref, n_valid):
        sl = slice(c * DK_A, (c + 1) * DK_A)
        s = lax.dot_general(q_ref[0, :, sl], keys_ref[0, :, sl], NT_DIMS, preferred_element_type=F32)
        col = lax.broadcasted_iota(jnp.int32, (tq, LANES), 1)
        return jnp.where(col < n_valid, s, -jnp.inf)

    @pl.when(ki == 0)
    def _meta():
        for c in range(2):
            s = small_scores(c, km_ref, N_META)
            m = jnp.broadcast_to(jnp.max(s, axis=1, keepdims=True), (tq, LANES))
            p = jnp.exp2(s - m)
            m_ref[c] = m
            l_ref[c] = p
            acc_ref[c] = jnp.dot(p.astype(BF16), vm_ref[0], preferred_element_type=F32)

    def update(c, rows, s, values, width):
        piece = PV_PIECE if width % PV_PIECE == 0 else LANES
        m_old = m_ref[c, rows, :]
        m_new = jnp.maximum(m_old, jnp.max(s, axis=1, keepdims=True))
        alpha = jnp.exp2(m_old - m_new)
        lsum = alpha * l_ref[c, rows, :]
        pv = None
        for j in range(width // piece):
            slabs = []
            for jj in range(piece // LANES):
                lo = j * piece + jj * LANES
                pj = jnp.exp2(s[:, lo:lo + LANES] - m_new)
                lsum = lsum + pj
                slabs.append(pj.astype(BF16))
            d = jnp.dot(jnp.concatenate(slabs, axis=1), values(j * piece, piece), preferred_element_type=F32)
            pv = d if pv is None else pv + d
        l_ref[c, rows, :] = lsum
        m_ref[c, rows, :] = m_new
        acc_ref[c, rows, :] = jnp.concatenate([alpha] * (DV_A // LANES), axis=1) * acc_ref[c, rows, :] + pv

    def chain(c, rows, mask):
        sl = slice(c * DK_A, (c + 1) * DK_A)
        s = lax.dot_general(q_ref[0, rows, sl], k_ref[0, :, sl].astype(BF16), NT_DIMS,
                            preferred_element_type=F32)
        return s if mask is None else jnp.where(mask, s, -jnp.inf)

    def block(mask_fn):
        band = tq // row_bands
        chains = [(c, slice(r * band, (r + 1) * band), None if mask_fn is None else mask_fn(r * band, band))
                  for r in range(row_bands) for c in range(2)]
        scores = [chain(*ch) for ch in chains]
        for (c, rows, _), s in zip(chains, scores):
            update(c, rows, s, lambda lo, n: v_ref[0, lo:lo + n, :].astype(BF16), tk)

    if causal:
        @pl.when(ki != qi)
        def _full():
            block(None)

        @pl.when(ki == qi)
        def _diag():
            def chunk_mask(row0, band):
                row = lax.broadcasted_iota(jnp.int32, (band, tk), 0) + row0
                col = lax.broadcasted_iota(jnp.int32, (band, tk), 1)
                return col // CHUNK <= row // CHUNK
            block(chunk_mask)
        last = ki == qi
    else:
        block(None)
        last = ki == n_kv - 1

    @pl.when(last)
    def _finish():
        if tail_valid:
            for c in range(2):
                update(c, slice(0, tq), small_scores(c, kt_ref, tail_valid),
                       lambda lo, n: vt_ref[0, lo:lo + n, :], LANES)
        lp = lam_ref[...]
        lam = (jnp.exp(jnp.sum(lp[0:1] * lp[1:2], axis=1, keepdims=True))
               - jnp.exp(jnp.sum(lp[2:3] * lp[3:4], axis=1, keepdims=True)) + LAM_INIT)
        l0 = jnp.sum(l_ref[0], axis=1, keepdims=True)
        l1 = jnp.sum(l_ref[1], axis=1, keepdims=True)
        o = acc_ref[0] / l0 - lam * (acc_ref[1] / l1)
        ms = jnp.mean(o * o, axis=1, keepdims=True)
        o_ref[0] = (o * lax.rsqrt(ms + LN_EPS) * g_ref[...] * (1.0 - LAM_INIT)).astype(o_ref.dtype)


def _diff_attn(q, k, v, k_meta, v_meta, k_tail, v_tail, lam_qk, subln_g, *, causal, tail_valid, tq, tk):
    bsz, lq, _ = q.shape
    lk = k.shape[1]
    assert lq % tq == 0 and lk % tk == 0 and tk % LANES == 0, (lq, tq, lk, tk)
    nq, nk = lq // tq, lk // tk
    if causal:
        pairs = [(i, j) for i in range(nq) for j in range(i + 1)]
    else:
        pairs = [(i, j) for i in range(nq) for j in range(nk)]
    qi = jnp.asarray([p[0] for p in pairs], jnp.int32)
    ki = jnp.asarray([p[1] for p in pairs], jnp.int32)
    per_batch_tail = k_tail.shape[0] > 1
    tail_map = lambda b, h, s, qi, ki: (b if per_batch_tail else 0, 0, h)
    meta_map = lambda b, h, s, qi, ki: (0, 0, h)
    grid_spec = pltpu.PrefetchScalarGridSpec(
        num_scalar_prefetch=2,
        grid=(bsz, H_A, len(pairs)),
        in_specs=[
            pl.BlockSpec((1, tq, HEAD_A), lambda b, h, s, qi, ki: (b, qi[s], h)),
            pl.BlockSpec((1, tk, HEAD_A), lambda b, h, s, qi, ki: (b, ki[s], h)),
            pl.BlockSpec((1, tk, HEAD_A), lambda b, h, s, qi, ki: (b, ki[s], h)),
            pl.BlockSpec((1, LANES, HEAD_A), meta_map),
            pl.BlockSpec((1, LANES, HEAD_A), meta_map),
            pl.BlockSpec((1, LANES, HEAD_A), tail_map),
            pl.BlockSpec((1, LANES, HEAD_A), tail_map),
            pl.BlockSpec((4, DK_A), lambda b, h, s, qi, ki: (0, 0)),
            pl.BlockSpec((1, DV_A), lambda b, h, s, qi, ki: (0, 0)),
        ],
        out_specs=pl.BlockSpec((1, tq, HEAD_A), lambda b, h, s, qi, ki: (b, qi[s], h)),
        scratch_shapes=[pltpu.VMEM((2, tq, LANES), F32), pltpu.VMEM((2, tq, LANES), F32),
                        pltpu.VMEM((2, tq, DV_A), F32)],
    )
    return pl.pallas_call(
        functools.partial(_attn_kernel, tq=tq, tk=tk, causal=causal, tail_valid=tail_valid, n_kv=nk),
        grid_spec=grid_spec,
        out_shape=jax.ShapeDtypeStruct((bsz, lq, H_A * HEAD_A), BF16),
        compiler_params=_params("parallel", "parallel", "arbitrary"),
        name="diff_attn_causal" if causal else "diff_attn_cache",
    )(qi, ki, q, k, v, k_meta, v_meta, k_tail, v_tail, lam_qk, subln_g)


def _hgrn_kernel(q_ref, k_ref, lf_ref, v_ref, gr_ref, s0_ref, g_ref, sel_ref, o_ref, sout_ref, st_ref, *, tb):
    t = pl.program_id(2)

    @pl.when(t == 0)
    def _load_state():
        st_ref[...] = s0_ref[0, 0].T

    row = lax.broadcasted_iota(jnp.int32, (CHUNK, DK_R), 0)
    sub = row // SUB
    tin = row % SUB
    same_sub = (lax.broadcasted_iota(jnp.int32, (CHUNK, CHUNK), 0) // SUB
                == lax.broadcasted_iota(jnp.int32, (CHUNK, CHUNK), 1) // SUB)
    g = g_ref[...]
    nsub = CHUNK // SUB

    def chunk(c, carry):
        sl = pl.ds(pl.multiple_of(c * CHUNK, CHUNK), CHUNK)
        qf = q_ref[0, sl, :].astype(F32)
        kf = k_ref[0, sl, :].astype(F32)
        vb = v_ref[0, sl, :]
        b = lf_ref[0, sl, :]
        sh = 1
        while sh < CHUNK:
            b = b + jnp.where(row >= sh, pltpu.roll(b, sh, 0), 0.0)
            sh *= 2
        b = b * LOG2_E
        b_last = b[CHUNK - 1:CHUNK, :]
        st = st_ref[...]
        qdec = (qf * jnp.exp2(b)).astype(BF16)
        o = lax.dot_general(qdec, st.astype(BF16), NT_DIMS, preferred_element_type=F32)
        ends = [b[SUB * j + SUB - 1:SUB * j + SUB, :] for j in range(nsub)]
        end_of_sub = jnp.concatenate([jnp.broadcast_to(e, (SUB, DK_R)) for e in ends], axis=0)
        kk = kf * jnp.exp2(end_of_sub - b)
        qs, ks = [], []
        for j in range(nsub - 1):
            e = jnp.exp2(jnp.where(sub > j, b - ends[j], -jnp.inf))
            qs.append((qf * e).astype(BF16))
            ks.append(jnp.where(sub == j, kk, 0.0).astype(BF16))
        sc = lax.dot_general(jnp.concatenate(qs, axis=1), jnp.concatenate(ks, axis=1), NT_DIMS,
                             preferred_element_type=F32)
        k3 = kf.reshape(nsub, SUB, DK_R)
        b3 = b.reshape(nsub, SUB, DK_R)
        prods = []
        for s in range(SUB):
            shp = (nsub, SUB, DK_R)
            ks_b = jnp.broadcast_to(k3[:, s:s + 1, :], shp).reshape(CHUNK, DK_R)
            bs_b = jnp.broadcast_to(b3[:, s:s + 1, :], shp).reshape(CHUNK, DK_R)
            db = b - bs_b
            a = qf * ks_b * jnp.exp2(db if s == 0 else jnp.where(tin >= s, db, -jnp.inf))
            prods.append(a.astype(BF16))
        sd = jnp.dot(jnp.concatenate(prods, axis=1), sel_ref[...], preferred_element_type=F32)
        sc = sc + jnp.where(same_sub, sd, 0.0)
        o = o + jnp.dot(sc.astype(BF16), vb, preferred_element_type=F32)
        kdec = (kf * jnp.exp2(b_last - b)).astype(BF16)
        st_ref[...] = jnp.exp2(b_last) * st + lax.dot_general(vb, kdec, TN_DIMS, preferred_element_type=F32)
        ms = jnp.mean(o * o, axis=1, keepdims=True)
        gr = gr_ref[0, sl, :]
        o_ref[0, sl, :] = (o * lax.rsqrt(ms + LN_EPS) * g * (gr * jax.nn.sigmoid(gr))).astype(o_ref.dtype)
        return carry

    lax.fori_loop(0, tb // CHUNK, chunk, 0, unroll=min(HGRN_UNROLL, tb // CHUNK))

    @pl.when(t == pl.num_programs(2) - 1)
    def _store_state():
        sout_ref[0, 0] = st_ref[...].T


def _hgrn(q, k, lf, v, g3, s0, rnorm_g):
    bsz, length, _ = q.shape
    tb = _tile(length, 1024, CHUNK)
    tok = lambda b, h, t: (b, t, h)
    sel = (lax.broadcasted_iota(jnp.int32, (SUB * DK_R, CHUNK), 0) // DK_R
           == lax.broadcasted_iota(jnp.int32, (SUB * DK_R, CHUNK), 1) % SUB).astype(BF16)
    return pl.pallas_call(
        functools.partial(_hgrn_kernel, tb=tb),
        grid=(bsz, H_R, length // tb),
        in_specs=[pl.BlockSpec((1, tb, DK_R), tok), pl.BlockSpec((1, tb, DK_R), tok),
                  pl.BlockSpec((1, tb, DK_R), tok), pl.BlockSpec((1, tb, DV_R), tok),
                  pl.BlockSpec((1, tb, DV_R), tok),
                  pl.BlockSpec((1, 1, DK_R, DV_R), lambda b, h, t: (b, h, 0, 0)),
                  pl.BlockSpec((1, DV_R), lambda b, h, t: (0, 0)),
                  pl.BlockSpec((SUB * DK_R, CHUNK), lambda b, h, t: (0, 0))],
        out_specs=[pl.BlockSpec((1, tb, DV_R), tok),
                   pl.BlockSpec((1, 1, DK_R, DV_R), lambda b, h, t: (b, h, 0, 0))],
        out_shape=[jax.ShapeDtypeStruct((bsz, length, D_MODEL), BF16),
                   jax.ShapeDtypeStruct((bsz, H_R, DK_R, DV_R), F32)],
        scratch_shapes=[pltpu.VMEM((DV_R, DK_R), F32)],
        compiler_params=_params("parallel", "parallel", "arbitrary"),
        name="hgrn2",
    )(q, k, lf, v, g3, s0, rnorm_g, sel)


def _merge_kernel(a_ref, r_ref, wa_ref, wb_ref, ga_ref, gb_ref, o_ref):
    ya = jnp.dot(a_ref[...], wa_ref[...], preferred_element_type=F32)
    yr = jnp.dot(r_ref[...], wb_ref[...], preferred_element_type=F32)
    o_ref[...] = (jax.nn.sigmoid(ga_ref[...]) * ya + jax.nn.sigmoid(gb_ref[...]) * yr).astype(o_ref.dtype)


def _merge(a_n, r_n, w_pa, w_pb, g3):
    m = a_n.shape[0]
    tm = _tile(m, 512, 16)
    tn = 1024
    nj = D_MODEL // tn
    return pl.pallas_call(
        _merge_kernel,
        grid=(m // tm, nj),
        in_specs=[pl.BlockSpec((tm, D_MODEL), lambda i, j: (i, 0)),
                  pl.BlockSpec((tm, D_MODEL), lambda i, j: (i, 0)),
                  pl.BlockSpec((D_MODEL, tn), lambda i, j: (0, j)),
                  pl.BlockSpec((D_MODEL, tn), lambda i, j: (0, j)),
                  pl.BlockSpec((tm, tn), lambda i, j: (i, j + nj)),
                  pl.BlockSpec((tm, tn), lambda i, j: (i, j + 2 * nj))],
        out_specs=pl.BlockSpec((tm, tn), lambda i, j: (i, j)),
        out_shape=jax.ShapeDtypeStruct((m, D_MODEL), BF16),
        compiler_params=_params("parallel", "parallel"),
        name="merge",
    )(a_n, r_n, w_pa, w_pb, g3, g3)


def _post_kernel(m_ref, wo_ref, x_ref, g0_ref, b0_ref, g1_ref, b1_ref, wr_ref, br_ref,
                 h32_ref, hp_ref, te_ref, tg_ref, *, band):
    for r0 in range(0, m_ref.shape[0], band):
        rows = slice(r0, r0 + band)
        mix = jnp.dot(m_ref[rows, :], wo_ref[...], preferred_element_type=F32)
        x = _layer_norm(x_ref[rows, :], g0_ref[...], b0_ref[...])
        h = _layer_norm(DEEPNORM_ALPHA * x + mix, g1_ref[...], b1_ref[...])
        h32_ref[rows, :] = h
        _store_packed(hp_ref.at[pl.ds(r0 * ROW_SUB, band * ROW_SUB)], h, band)
        logits = jnp.dot(h, wr_ref[...], preferred_element_type=F32, precision=lax.Precision.HIGHEST)
        logits = logits + br_ref[...]
        col = lax.broadcasted_iota(jnp.int32, logits.shape, 1)
        vals, idxs = [], []
        for _ in range(TOP_K):
            mx = jnp.max(logits, axis=1, keepdims=True)
            idx = jnp.min(jnp.where(logits == mx, col, LANES), axis=1, keepdims=True)
            vals.append(mx)
            idxs.append(idx)
            logits = jnp.where(col == idx, -jnp.inf, logits)
        es = [jnp.exp(v - vals[0]) for v in vals]
        den = es[0] + es[1] + es[2] + es[3]
        te = jnp.zeros(col.shape, jnp.int32)
        tg = jnp.zeros(col.shape, F32)
        for kk in range(TOP_K):
            te = jnp.where(col == kk, idxs[kk], te)
            tg = jnp.where(col == kk, es[kk] / den, tg)
        te_ref[rows, :] = te
        tg_ref[rows, :] = tg


def _post(mrg, w_o, x, ln0_g, ln0_b, ln1_g, ln1_b, w_r, b_r):
    m = mrg.shape[0]
    tm = _tile(m, 512, 16)
    band = _tile(tm, 256, 16)
    row = lambda i: (i, 0)
    fix = lambda i: (0, 0)
    once = dict(pipeline_mode=pl.Buffered(1))
    return pl.pallas_call(
        functools.partial(_post_kernel, band=band),
        grid=(m // tm,),
        in_specs=[pl.BlockSpec((tm, D_MODEL), row), pl.BlockSpec((D_MODEL, D_MODEL), fix, **once),
                  pl.BlockSpec((tm, D_MODEL), row),
                  pl.BlockSpec((1, D_MODEL), fix), pl.BlockSpec((1, D_MODEL), fix),
                  pl.BlockSpec((1, D_MODEL), fix), pl.BlockSpec((1, D_MODEL), fix),
                  pl.BlockSpec((D_MODEL, LANES), fix, **once), pl.BlockSpec((1, LANES), fix)],
        out_specs=[pl.BlockSpec((tm, D_MODEL), row), pl.BlockSpec((tm * ROW_SUB, LANES), row),
                   pl.BlockSpec((tm, LANES), row), pl.BlockSpec((tm, LANES), row)],
        out_shape=[jax.ShapeDtypeStruct((m, D_MODEL), F32), jax.ShapeDtypeStruct((m * ROW_SUB, LANES), jnp.uint32),
                   jax.ShapeDtypeStruct((m, LANES), jnp.int32), jax.ShapeDtypeStruct((m, LANES), F32)],
        compiler_params=_params("parallel"),
        name="post",
    )(mrg, w_o, x, ln0_g, ln0_b, ln1_g, ln1_b, w_r, b_r)


def _deint_kernel(w_ref, p_ref, g_ref, l_ref, *, tc):
    y = jnp.dot(w_ref[0].astype(BF16), p_ref[...], preferred_element_type=F32)
    g_ref[0] = y[:, :tc].astype(BF16)
    l_ref[0] = y[:, tc:].astype(BF16)


def _deinterleave(w1):
    tc, tr = 512, 1024
    r = lax.broadcasted_iota(jnp.int32, (2 * tc, 2 * tc), 0)
    c = lax.broadcasted_iota(jnp.int32, (2 * tc, 2 * tc), 1)
    perm = jnp.where(c < tc, r == 2 * c, r == 2 * (c - tc) + 1).astype(BF16)
    out = jax.ShapeDtypeStruct((N_EXPERTS, D_MODEL, D_FF), BF16)
    return pl.pallas_call(
        functools.partial(_deint_kernel, tc=tc),
        grid=(N_EXPERTS, D_MODEL // tr, D_FF // tc),
        in_specs=[pl.BlockSpec((1, tr, 2 * tc), lambda e, i, j: (e, i, j)),
                  pl.BlockSpec((2 * tc, 2 * tc), lambda e, i, j: (0, 0))],
        out_specs=[pl.BlockSpec((1, tr, tc), lambda e, i, j: (e, i, j)),
                   pl.BlockSpec((1, tr, tc), lambda e, i, j: (e, i, j))],
        out_shape=[out, out],
        compiler_params=_params("parallel", "parallel", "parallel"),
        name="w1_deinterleave",
    )(w1, perm)


ROW_SUB = 8
HALF = D_MODEL // 2


def _store_packed(o_ref, x, n):
    bits = lax.bitcast_convert_type(x.astype(BF16).astype(F32), jnp.uint32)
    for c in range(ROW_SUB):
        lo = bits[:, c * LANES:(c + 1) * LANES]
        hi = bits[:, HALF + c * LANES:HALF + (c + 1) * LANES]
        o_ref[pl.ds(c, n, stride=ROW_SUB), :] = (lo >> 16) | hi


def _load_packed(x_ref, c, n):
    w = x_ref[pl.ds(c, n, stride=ROW_SUB), :]
    lo = lax.bitcast_convert_type(w << 16, F32)
    hi = lax.bitcast_convert_type(w & jnp.uint32(0xFFFF0000), F32)
    return lo, hi


def _gather_kernel(idx_ref, src_ref, o_ref, sem, *, g):
    def row_copy(r, src_row):
        return pltpu.make_async_copy(src_ref.at[pl.ds(pl.multiple_of(src_row * ROW_SUB, ROW_SUB), ROW_SUB)],
                                     o_ref.at[pl.ds(pl.multiple_of(r * ROW_SUB, ROW_SUB), ROW_SUB)], sem)

    def issue(r, carry):
        row_copy(r, idx_ref[r]).start()
        return carry

    def drain(r, carry):
        row_copy(r, 0).wait()
        return carry

    lax.fori_loop(0, g, issue, 0, unroll=8)
    lax.fori_loop(0, g, drain, 0, unroll=8)


def _gather_rows(src, idx):
    n = idx.shape[0]
    g = _tile(n, 512, 128)
    return pl.pallas_call(
        functools.partial(_gather_kernel, g=g),
        grid=(n // g,),
        in_specs=[pl.BlockSpec((g,), lambda i: (i,), memory_space=pltpu.SMEM),
                  pl.BlockSpec(memory_space=pl.ANY)],
        out_specs=pl.BlockSpec((g * ROW_SUB, LANES), lambda i: (i, 0)),
        out_shape=jax.ShapeDtypeStruct((n * ROW_SUB, LANES), jnp.uint32),
        scratch_shapes=[pltpu.SemaphoreType.DMA(())],
        compiler_params=_params("arbitrary"),
        name="gather_rows",
    )(idx, src)


def _expert_kernel(be_ref, nb_ref, idx0_ref, idxn_ref, hp_ref, wg_ref, wl_ref, bg_ref, bl_ref, w2_ref, b2_ref,
                   o_ref, acc_ref, x16_ref, xbuf_ref, sem, *, nf):
    i = pl.program_id(0)
    f = pl.program_id(1)
    nblk = pl.num_programs(0)
    used = i < nb_ref[0]
    blk = x16_ref.shape[0]
    per = blk // nf
    slot = i % 2
    nxt = 1 - slot

    def row_copy(s, r, src_row):
        return pltpu.make_async_copy(hp_ref.at[pl.ds(pl.multiple_of(src_row * ROW_SUB, ROW_SUB), ROW_SUB)],
                                     xbuf_ref.at[s, pl.ds(pl.multiple_of(r * ROW_SUB, ROW_SUB), ROW_SUB)],
                                     sem.at[s])

    def drain(s):
        def body(r, carry):
            row_copy(s, r, 0).wait()
            return carry
        lax.fori_loop(0, blk, body, 0, unroll=8)

    @pl.when(jnp.logical_and(i == 0, f == 0))
    def _first_rows():
        def body(r, carry):
            row_copy(0, r, idx0_ref[r]).start()
            return carry
        lax.fori_loop(0, blk, body, 0, unroll=8)

    def prefetch_next():
        for u in range(per):
            r = f * per + u
            row_copy(nxt, r, idxn_ref[r]).start()

    @pl.when(f == 0)
    def _rows_ready():
        drain(slot)

    @pl.when(jnp.logical_and(used, f == 0))
    def _unpack():
        x_ref = xbuf_ref.at[slot]
        for c in range(ROW_SUB):
            lo, hi = _load_packed(x_ref, c, blk)
            x16_ref[:, c * LANES:(c + 1) * LANES] = lo.astype(BF16)
            x16_ref[:, HALF + c * LANES:HALF + (c + 1) * LANES] = hi.astype(BF16)
        acc_ref[...] = jnp.broadcast_to(b2_ref[0], acc_ref.shape)

    @pl.when(used)
    def _compute():
        prefetch_next()
        x = x16_ref[...]
        hg = jnp.dot(x, wg_ref[0], preferred_element_type=F32) + bg_ref[0]
        hl = jnp.dot(x, wl_ref[0], preferred_element_type=F32) + bl_ref[0]
        glu = jnp.minimum(hg, SWIGLU_LIMIT)
        lin = jnp.clip(hl, -SWIGLU_LIMIT, SWIGLU_LIMIT)
        act = (glu * jax.nn.sigmoid(SWIGLU_ALPHA * glu) * (lin + 1.0)).astype(BF16)
        acc_ref[...] += jnp.dot(act, w2_ref[0], preferred_element_type=F32)

        @pl.when(f == nf - 1)
        def _store():
            _store_packed(o_ref, acc_ref[...], blk)

    @pl.when(jnp.logical_not(used))
    def _unused():
        prefetch_next()

        @pl.when(f == nf - 1)
        def _zero():
            o_ref[...] = jnp.zeros(o_ref.shape, o_ref.dtype)

    @pl.when(jnp.logical_and(i == nblk - 1, f == nf - 1))
    def _last_rows():
        drain(nxt)


def _experts(hp, row_tok, blk_e, n_used, w1g, w1l, b1g, b1l, w2, b2, blk):
    rows = row_tok.shape[0]
    nblk = rows // blk
    tf = 1024
    nf = D_FF // tf
    grid_spec = pltpu.PrefetchScalarGridSpec(
        num_scalar_prefetch=2,
        grid=(nblk, nf),
        in_specs=[
            pl.BlockSpec((blk,), lambda i, f, be, nb: (0,), memory_space=pltpu.SMEM),
            pl.BlockSpec((blk,), lambda i, f, be, nb: (jnp.minimum(i + 1, nblk - 1),), memory_space=pltpu.SMEM),
            pl.BlockSpec(memory_space=pl.ANY),
            pl.BlockSpec((1, D_MODEL, tf), lambda i, f, be, nb: (be[i], 0, f)),
            pl.BlockSpec((1, D_MODEL, tf), lambda i, f, be, nb: (be[i], 0, f)),
            pl.BlockSpec((1, 1, tf), lambda i, f, be, nb: (be[i], 0, f)),
            pl.BlockSpec((1, 1, tf), lambda i, f, be, nb: (be[i], 0, f)),
            pl.BlockSpec((1, tf, D_MODEL), lambda i, f, be, nb: (be[i], f, 0)),
            pl.BlockSpec((1, 1, D_MODEL), lambda i, f, be, nb: (be[i], 0, 0)),
        ],
        out_specs=pl.BlockSpec((blk * ROW_SUB, LANES), lambda i, f, be, nb: (i, 0)),
        scratch_shapes=[pltpu.VMEM((blk, D_MODEL), F32), pltpu.VMEM((blk, D_MODEL), BF16),
                        pltpu.VMEM((2, blk * ROW_SUB, LANES), jnp.uint32), pltpu.SemaphoreType.DMA((2,))],
    )
    return pl.pallas_call(
        functools.partial(_expert_kernel, nf=nf),
        grid_spec=grid_spec,
        out_shape=jax.ShapeDtypeStruct((rows * ROW_SUB, LANES), jnp.uint32),
        compiler_params=_params("arbitrary", "arbitrary"),
        name="experts",
    )(blk_e, n_used, row_tok, row_tok, hp, w1g, w1l, b1g, b1l, w2, b2)


def _combine_kernel(h_ref, y_ref, tg_ref, g_ref, b_ref, o_ref):
    tm = h_ref.shape[0]
    tg = tg_ref[...]
    los, his = [], []
    for c in range(ROW_SUB):
        lo = hi = None
        for kk in range(TOP_K):
            ylo, yhi = _load_packed(y_ref.at[kk], c, tm)
            gate = tg[:, kk:kk + 1]
            lo = gate * ylo if lo is None else lo + gate * ylo
            hi = gate * yhi if hi is None else hi + gate * yhi
        los.append(lo)
        his.append(hi)
    f = jnp.concatenate(los + his, axis=1)
    o_ref[...] = _layer_norm(DEEPNORM_ALPHA * h_ref[...] + f, g_ref[...], b_ref[...])


def _combine(h32, yg, tg, ln2_g, ln2_b, row_off):
    m = h32.shape[0]
    tm = _tile(m, 256, 16)
    off = row_off // tm
    return pl.pallas_call(
        _combine_kernel,
        grid=(m // tm,),
        in_specs=[pl.BlockSpec((tm, D_MODEL), lambda i: (i, 0)),
                  pl.BlockSpec((TOP_K, tm * ROW_SUB, LANES), lambda i: (0, i + off, 0)),
                  pl.BlockSpec((tm, LANES), lambda i: (i, 0)),
                  pl.BlockSpec((1, D_MODEL), lambda i: (0, 0)),
                  pl.BlockSpec((1, D_MODEL), lambda i: (0, 0))],
        out_specs=pl.BlockSpec((tm, D_MODEL), lambda i: (i, 0)),
        out_shape=jax.ShapeDtypeStruct((m, D_MODEL), F32),
        compiler_params=_params("parallel"),
        name="combine",
    )(h32, yg, tg, ln2_g, ln2_b)


def _route(top_e, blk):
    t = top_e.shape[0]
    n = t * TOP_K
    flat_e = top_e.reshape(-1)
    onehot = (flat_e[:, None] == jnp.arange(N_EXPERTS, dtype=jnp.int32)[None, :]).astype(jnp.int32)
    csum = jnp.cumsum(onehot, axis=0)
    rank = jnp.sum(onehot * csum, axis=1) - 1
    counts = csum[-1]
    padded = (counts + blk - 1) // blk * blk
    pad_end = jnp.cumsum(padded)
    pad_start = pad_end - padded
    start = jnp.cumsum(counts) - counts
    dest = (pad_start[flat_e] + rank).astype(jnp.int32)
    n_blocks = -(-n // blk) + N_EXPERTS
    blk_e = jnp.minimum(jnp.searchsorted(pad_end, jnp.arange(n_blocks, dtype=jnp.int32) * blk, side="right"),
                        N_EXPERTS - 1).astype(jnp.int32)
    order = jnp.argsort(flat_e, stable=True).astype(jnp.int32)
    r = jnp.arange(n_blocks * blk, dtype=jnp.int32)
    row_e = jnp.repeat(blk_e, blk)
    j = r - pad_start[row_e]
    valid = j < counts[row_e]
    row_tok = jnp.where(valid, order[jnp.clip(start[row_e] + j, 0, n - 1)] // TOP_K, 0).astype(jnp.int32)
    n_used = (pad_end[-1:] // blk).astype(jnp.int32)
    return row_tok, blk_e, n_used, dest.reshape(t, TOP_K)


def _pad_rows(a, n):
    return jnp.pad(a, ((0, n - a.shape[0]),) + ((0, 0),) * (a.ndim - 1))


def _forward(x_prompt, x_sample, cache_k, cache_v, state_hgrn, meta_tokens, ln0_g, ln0_b, w_in, lam_qk,
             subln_g, lb_logits, rnorm_g, w_pa, w_pb, w_o, ln1_g, ln1_b, ln2_g, ln2_b,
             w_router, b_router, w1, b1, w2, b2, *, expert_block, attn_tile):
    bsz, seq, _ = x_prompt.shape
    dec_b, dec_t, _ = x_sample.shape
    past = cache_k.shape[2]
    n_s = dec_b * dec_t
    n_p = bsz * seq

    row = lambda a: a.reshape(1, -1).astype(F32)
    g0, b0, g1, b1n, g2, b2n = row(ln0_g), row(ln0_b), row(ln1_g[0]), row(ln1_b[0]), row(ln2_g[0]), row(ln2_b[0])
    lb = jnp.cumsum(jax.nn.softmax(lb_logits.astype(F32), axis=0), axis=0)[0].reshape(1, -1)
    w_in16 = w_in[0].astype(BF16)
    w_pa16, w_pb16, w_o16 = w_pa[0].astype(BF16), w_pb[0].astype(BF16), w_o[0].astype(BF16)
    w_r = jnp.pad(w_router[0].astype(F32), ((0, 0), (0, LANES - N_EXPERTS)))
    b_r = jnp.pad(b_router[0].astype(F32), (0, LANES - N_EXPERTS), constant_values=-jnp.inf).reshape(1, LANES)
    w1g, w1l = _deinterleave(w1[0])
    b1g = b1[0, :, 0::2].reshape(N_EXPERTS, 1, D_FF).astype(F32)
    b1l = b1[0, :, 1::2].reshape(N_EXPERTS, 1, D_FF).astype(F32)
    w2_16 = w2[0].astype(BF16)
    b2r = b2[0].reshape(N_EXPERTS, 1, D_MODEL).astype(F32)
    lam = lam_qk[0].astype(F32)
    sub_g = subln_g[0].reshape(1, DV_A).astype(F32)
    rn_g = rnorm_g[0].reshape(1, DV_R).astype(F32)

    xs = jnp.concatenate([x_sample.reshape(n_s, D_MODEL), meta_tokens], axis=0)
    n_sm = n_s + N_META
    pos_s = jnp.concatenate([jnp.tile(N_META + past + jnp.arange(dec_t, dtype=jnp.int32), dec_b),
                             jnp.arange(N_META, dtype=jnp.int32)])
    cos_s, sin_s = _rope_tables(pos_s)
    hs16 = _ln_bf16(xs, g0, b0)
    q_s, k32_s, k16_s, v32_s, v16_s, qr_s, kr_s, lf_s, ir_s, g3_s = _in_proj(hs16, w_in16, lb, cos_s, sin_s, n_sm)
    k_meta32, v_meta32 = k32_s[n_s:], v32_s[n_s:]
    k_meta16 = _pad_rows(k16_s[n_s:], LANES)[None]
    v_meta16 = _pad_rows(v16_s[n_s:], LANES)[None]

    tail_pad = ((0, 0), (0, LANES - dec_t), (0, 0))
    k_new = jnp.pad(k16_s[:n_s].reshape(dec_b, dec_t, D_MODEL), tail_pad)
    v_new = jnp.pad(v16_s[:n_s].reshape(dec_b, dec_t, D_MODEL), tail_pad)
    a_s = _diff_attn(q_s[:n_s].reshape(dec_b, dec_t, D_MODEL),
                     cache_k[0].reshape(dec_b, past, D_MODEL), cache_v[0].reshape(dec_b, past, D_MODEL),
                     k_meta16, v_meta16, k_new, v_new, lam, sub_g,
                     causal=False, tail_valid=dec_t, tq=dec_t, tk=_tile(past, 1024, LANES))

    n_str = dec_b + 1
    chunk_pad = ((0, 0), (0, CHUNK - dec_t), (0, 0))
    to_chunk = lambda a: jnp.pad(a.reshape(n_str, dec_t, a.shape[-1]), chunk_pad)
    s0 = jnp.concatenate([state_hgrn[0].astype(F32), jnp.zeros((1, H_R, DK_R, DV_R), F32)], axis=0)
    r_s, s_out = _hgrn(to_chunk(qr_s), to_chunk(kr_s), to_chunk(lf_s), to_chunk(ir_s), to_chunk(g3_s), s0, rn_g)
    s_sample, s_meta = s_out[:dec_b], s_out[dec_b:]

    mrg_s = _merge(a_s.reshape(n_s, D_MODEL), r_s[:dec_b, :dec_t].reshape(n_s, D_MODEL), w_pa16, w_pb16, g3_s[:n_s])
    h32_s, hp_s, te_s, tg_s = _post(mrg_s, w_o16, x_sample.reshape(n_s, D_MODEL), g0, b0, g1, b1n, w_r, b_r)

    xp = x_prompt.reshape(n_p, D_MODEL)
    cos_p, sin_p = _rope_tables(N_META + jnp.arange(seq, dtype=jnp.int32))
    hp16 = _ln_bf16(xp, g0, b0)
    q_p, k32_p, k16_p, v32_p, v16_p, qr_p, kr_p, lf_p, ir_p, g3_p = _in_proj(hp16, w_in16, lb, cos_p, sin_p, seq)
    t3 = lambda a: a.reshape(bsz, seq, a.shape[-1])
    ta = _tile(seq, attn_tile, CHUNK)
    a_p = _diff_attn(t3(q_p), t3(k16_p), t3(v16_p), k_meta16, v_meta16, k_meta16, v_meta16, lam, sub_g,
                     causal=True, tail_valid=0, tq=ta, tk=ta)
    r_p, s_prompt = _hgrn(t3(qr_p), t3(kr_p), t3(lf_p), t3(ir_p), t3(g3_p),
                          jnp.broadcast_to(s_meta, (bsz, H_R, DK_R, DV_R)), rn_g)
    mrg_p = _merge(a_p.reshape(n_p, D_MODEL), r_p.reshape(n_p, D_MODEL), w_pa16, w_pb16, g3_p)
    h32_p, hp_p, te_p, tg_p = _post(mrg_p, w_o16, xp, g0, b0, g1, b1n, w_r, b_r)

    hp = jnp.concatenate([hp_p, hp_s], axis=0)
    top_e = jnp.concatenate([te_p[:, :TOP_K], te_s[:, :TOP_K]], axis=0)
    row_tok, blk_e, n_used, dest = _route(top_e, expert_block)
    yb = _experts(hp, row_tok, blk_e, n_used, w1g, w1l, b1g, b1l, w2_16, b2r, expert_block)
    yg = _gather_rows(yb, dest.T.reshape(-1)).reshape(TOP_K, (n_p + n_s) * ROW_SUB, LANES)
    y_p = _combine(h32_p, yg, tg_p, g2, b2n, 0)
    y_s = _combine(h32_s, yg, tg_s, g2, b2n, n_p)

    k_prompt = jnp.concatenate([jnp.broadcast_to(k_meta32[None], (bsz, N_META, D_MODEL)), t3(k32_p)], axis=1)
    v_prompt = jnp.concatenate([jnp.broadcast_to(v_meta32[None], (bsz, N_META, D_MODEL)), t3(v32_p)], axis=1)
    return (y_p.reshape(bsz, seq, D_MODEL),
            y_s.reshape(dec_b, dec_t, D_MODEL),
            k_prompt.reshape(1, bsz, N_META + seq, H_A, 2, DK_A),
            v_prompt.reshape(1, bsz, N_META + seq, H_A, DV_A),
            s_prompt[None],
            k32_s[:n_s].reshape(1, dec_b, dec_t, H_A, 2, DK_A),
            v32_s[:n_s].reshape(1, dec_b, dec_t, H_A, DV_A),
            s_sample[None])


def kernel(x_prompt, x_sample, cache_k, cache_v, state_hgrn, meta_tokens, ln0_g, ln0_b, w_in, lam_qk, subln_g, lb_logits, rnorm_g, w_pa, w_pb, w_o, ln1_g, ln1_b, ln2_g, ln2_b, w_router, b_router, w1, b1, w2, b2):
    return _forward(x_prompt, x_sample, cache_k, cache_v, state_hgrn, meta_tokens, ln0_g, ln0_b, w_in, lam_qk,
                    subln_g, lb_logits, rnorm_g, w_pa, w_pb, w_o, ln1_g, ln1_b, ln2_g, ln2_b,
                    w_router, b_router, w1, b1, w2, b2, expert_block=512, attn_tile=1024)
```

```python
import functools
import math

import jax
import jax.numpy as jnp
from jax import lax
from jax.experimental import pallas as pl
from jax.experimental.pallas import tpu as pltpu

F32 = jnp.float32
BF16 = jnp.bfloat16

D_MODEL = 2048
CHUNK = 64
SUB = 16
N_META = 16
H_A = 8
DK_A = 128
DV_A = 2 * DK_A
HEAD_A = 2 * DK_A
ROT_DIM = DK_A // 4
ROPE_THETA = 500000.0
H_R = 16
DK_R = D_MODEL // H_R
DV_R = D_MODEL // H_R
N_EXPERTS = 32
TOP_K = 4
D_FF = D_MODEL
SWIGLU_ALPHA = 1.702
SWIGLU_LIMIT = 7.0
LN_EPS = 1e-5
DEPTH = 1
DEEPNORM_ALPHA = (2.0 * DEPTH) ** 0.25
LAM_INIT = 0.8 - 0.6 * math.exp(-0.3 * 0)

SEG = D_MODEL
SEG_QA, SEG_KA, SEG_VA, SEG_QR, SEG_FR, SEG_IR, SEG_GR, SEG_GA, SEG_GB = range(9)

LOG2_E = math.log2(math.e)
Q_SCALE = DK_A ** -0.5 * LOG2_E
PV_PIECE = 256
HGRN_UNROLL = 16

LANES = 128
VMEM_LIMIT_BYTES = 56 * 1024 * 1024

NT_DIMS = (((1,), (1,)), ((), ()))
TN_DIMS = (((0,), (0,)), ((), ()))


def _params(*sem):
    return pltpu.CompilerParams(dimension_semantics=sem, vmem_limit_bytes=VMEM_LIMIT_BYTES)


def _tile(n, pref, mult):
    if n <= pref:
        return n
    t = pref - pref % mult
    while t > mult and n % t:
        t -= mult
    assert n % t == 0, (n, pref, mult)
    return t


def _layer_norm(x, g, b):
    mu = jnp.mean(x, axis=-1, keepdims=True)
    xc = x - mu
    var = jnp.mean(xc * xc, axis=-1, keepdims=True)
    return xc * lax.rsqrt(var + LN_EPS) * g + b


def _ln_kernel(x_ref, g_ref, b_ref, o_ref):
    o_ref[...] = _layer_norm(x_ref[...], g_ref[...], b_ref[...]).astype(o_ref.dtype)


def _ln_bf16(x, g, b):
    m = x.shape[0]
    tm = _tile(m, 512, 16)
    return pl.pallas_call(
        _ln_kernel,
        grid=(m // tm,),
        in_specs=[pl.BlockSpec((tm, D_MODEL), lambda i: (i, 0)),
                  pl.BlockSpec((1, D_MODEL), lambda i: (0, 0)),
                  pl.BlockSpec((1, D_MODEL), lambda i: (0, 0))],
        out_specs=pl.BlockSpec((tm, D_MODEL), lambda i: (i, 0)),
        out_shape=jax.ShapeDtypeStruct((m, D_MODEL), BF16),
        compiler_params=_params("parallel"),
        name="ln0",
    )(x, g, b)


def _rope_tables(pos):
    inv_freq = ROPE_THETA ** (-jnp.arange(0, ROT_DIM, 2, dtype=F32) / ROT_DIM)
    ang = pos.astype(F32)[:, None] * inv_freq[None, :]
    n = pos.shape[0]
    cos, sin = jnp.cos(ang), jnp.sin(ang)
    rest = LANES - ROT_DIM
    cos_t = jnp.concatenate([cos, cos, jnp.ones((n, rest), F32)], axis=1)
    sin_t = jnp.concatenate([-sin, sin, jnp.zeros((n, rest), F32)], axis=1)
    return cos_t, sin_t


def _proj_kernel(*refs, mode, tn):
    x_ref, w_ref = refs[:2]
    z = jnp.dot(x_ref[...], w_ref[...], preferred_element_type=F32)
    if mode == "plain":
        (o_ref,) = refs[2:]
        o_ref[...] = z.astype(o_ref.dtype)
    elif mode == "dual":
        o32_ref, o16_ref = refs[2:]
        o32_ref[...] = z
        o16_ref[...] = z.astype(BF16)
    elif mode == "gates":
        lb_ref, k_ref, lf_ref = refs[2:]
        lb = lb_ref[...]
        f = lb + (1.0 - lb) * jax.nn.sigmoid(z)
        k_ref[...] = (1.0 - f).astype(BF16)
        lf_ref[...] = jnp.log(f)
    else:
        cos_ref, sin_ref = refs[2:4]
        outs = refs[4:]
        cos = cos_ref[...]
        sin = sin_ref[...]
        lane = lax.broadcasted_iota(jnp.int32, cos.shape, 1)
        half = ROT_DIM // 2
        for c in range(tn // LANES):
            sl = slice(c * LANES, (c + 1) * LANES)
            zc = z[:, sl]
            partner = jnp.where(lane < half, pltpu.roll(zc, LANES - half, 1), pltpu.roll(zc, half, 1))
            r = zc * cos + partner * sin
            if mode == "rope_q":
                outs[0][:, sl] = (r * Q_SCALE).astype(BF16)
            else:
                outs[0][:, sl] = r
                outs[1][:, sl] = r.astype(BF16)


def _proj(hn, w, seg, nseg, mode, out_dtypes, period, extra=()):
    m = hn.shape[0]
    tm = _tile(period, 1024, 16)
    tn = 1024
    nj = nseg * SEG // tn
    off = seg * SEG // tn
    in_specs = [pl.BlockSpec((tm, D_MODEL), lambda i, j: (i, 0)),
                pl.BlockSpec((D_MODEL, tn), lambda i, j: (0, j + off))]
    if mode == "gates":
        in_specs.append(pl.BlockSpec((1, tn), lambda i, j: (0, j)))
    elif mode in ("rope_q", "rope_k"):
        nper = period // tm
        in_specs += [pl.BlockSpec((tm, LANES), lambda i, j: (i % nper, 0))] * 2
    out_specs = [pl.BlockSpec((tm, tn), lambda i, j: (i, j)) for _ in out_dtypes]
    out_shape = [jax.ShapeDtypeStruct((m, nseg * SEG), dt) for dt in out_dtypes]
    res = pl.pallas_call(
        functools.partial(_proj_kernel, mode=mode, tn=tn),
        grid=(m // tm, nj),
        in_specs=in_specs,
        out_specs=out_specs,
        out_shape=out_shape,
        compiler_params=_params("parallel", "parallel"),
        name="in_proj_" + mode,
    )(hn, w, *extra)
    return res


def _in_proj(hn, w_bf, lb, cos_t, sin_t, period):
    (q16,) = _proj(hn, w_bf, SEG_QA, 1, "rope_q", [BF16], period, (cos_t, sin_t))
    k32, k16 = _proj(hn, w_bf, SEG_KA, 1, "rope_k", [F32, BF16], period, (cos_t, sin_t))
    v32, v16 = _proj(hn, w_bf, SEG_VA, 1, "dual", [F32, BF16], period)
    (qr16,) = _proj(hn, w_bf, SEG_QR, 1, "plain", [BF16], period)
    kr16, lf32 = _proj(hn, w_bf, SEG_FR, 1, "gates", [BF16, F32], period, (lb,))
    (ir16,) = _proj(hn, w_bf, SEG_IR, 1, "plain", [BF16], period)
    (g32,) = _proj(hn, w_bf, SEG_GR, 3, "plain", [F32], period)
    return q16, k32, k16, v32, v16, qr16, kr16, lf32, ir16, g32


def _attn_kernel(qi_ref, ki_ref, q_ref, k_ref, v_ref, km_ref, vm_ref, kt_ref, vt_ref, lam_ref, g_ref, o_ref,
                 m_ref, l_ref, acc_ref, *, tq, tk, causal, tail_valid, n_kv):
    step = pl.program_id(2)
    qi = qi_ref[step]
    ki = ki_ref[step]
    row_bands = 2 if tq % 512 == 0 else 1

    def small_scores(c, keys_ref, n_valid):
        sl = slice(c * DK_A, (c + 1) * DK_A)
        s = lax.dot_general(q_ref[0, :, sl], keys_ref[0, :, sl], NT_DIMS, preferred_element_type=F32)
        col = lax.broadcasted_iota(jnp.int32, (tq, LANES), 1)
        return jnp.where(col < n_valid, s, -jnp.inf)

    @pl.when(ki == 0)
    def _meta():
        for c in range(2):
            s = small_scores(c, km_ref, N_META)
            m = jnp.broadcast_to(jnp.max(s, axis=1, keepdims=True), (tq, LANES))
            p = jnp.exp2(s - m)
            m_ref[c] = m
            l_ref[c] = p
            acc_ref[c] = jnp.dot(p.astype(BF16), vm_ref[0], preferred_element_type=F32)

    def update(c, rows, s, values, width):
        piece = PV_PIECE if width % PV_PIECE == 0 else LANES
        m_old = m_ref[c, rows, :]
        m_new = jnp.maximum(m_old, jnp.max(s, axis=1, keepdims=True))
        alpha = jnp.exp2(m_old - m_new)
        lsum = alpha * l_ref[c, rows, :]
        pv = None
        for j in range(width // piece):
            slabs = []
            for jj in range(piece // LANES):
                lo = j * piece + jj * LANES
                pj = jnp.exp2(s[:, lo:lo + LANES] - m_new)
                lsum = lsum + pj
                slabs.append(pj.astype(BF16))
            d = jnp.dot(jnp.concatenate(slabs, axis=1), values(j * piece, piece), preferred_element_type=F32)
            pv = d if pv is None else pv + d
        l_ref[c, rows, :] = lsum
        m_ref[c, rows, :] = m_new
        acc_ref[c, rows, :] = jnp.concatenate([alpha] * (DV_A // LANES), axis=1) * acc_ref[c, rows, :] + pv

    def chain(c, rows, width, mask):
        sl = slice(c * DK_A, (c + 1) * DK_A)
        s = lax.dot_general(q_ref[0, rows, sl], k_ref[0, 0:width, sl].astype(BF16), NT_DIMS,
                            preferred_element_type=F32)
        return s if mask is None else jnp.where(mask, s, -jnp.inf)

    def block(mask_fn, diagonal):
        band = tq // row_bands
        chains = []
        for r in range(row_bands):
            width = (r + 1) * band if diagonal else tk
            mask = None if mask_fn is None else mask_fn(r * band, band, width)
            chains += [(c, slice(r * band, (r + 1) * band), width, mask) for c in range(2)]
        scores = [chain(*ch) for ch in chains]
        for (c, rows, width, _), s in zip(chains, scores):
            update(c, rows, s, lambda lo, n: v_ref[0, lo:lo + n, :].astype(BF16), width)

    if causal:
        @pl.when(ki != qi)
        def _full():
            block(None, False)

        @pl.when(ki == qi)
        def _diag():
            def chunk_mask(row0, band, width):
                row = lax.broadcasted_iota(jnp.int32, (band, width), 0) + row0
                col = lax.broadcasted_iota(jnp.int32, (band, width), 1)
                return col // CHUNK <= row // CHUNK
            block(chunk_mask, True)
        last = ki == qi
    else:
        block(None, False)
        last = ki == n_kv - 1

    @pl.when(last)
    def _finish():
        if tail_valid:
            for c in range(2):
                update(c, slice(0, tq), small_scores(c, kt_ref, tail_valid),
                       lambda lo, n: vt_ref[0, lo:lo + n, :], LANES)
        lp = lam_ref[...]
        lam = (jnp.exp(jnp.sum(lp[0:1] * lp[1:2], axis=1, keepdims=True))
               - jnp.exp(jnp.sum(lp[2:3] * lp[3:4], axis=1, keepdims=True)) + LAM_INIT)
        l0 = jnp.sum(l_ref[0], axis=1, keepdims=True)
        l1 = jnp.sum(l_ref[1], axis=1, keepdims=True)
        o = acc_ref[0] / l0 - lam * (acc_ref[1] / l1)
        ms = jnp.mean(o * o, axis=1, keepdims=True)
        o_ref[0] = (o * lax.rsqrt(ms + LN_EPS) * g_ref[...] * (1.0 - LAM_INIT)).astype(o_ref.dtype)


def _diff_attn(q, k, v, k_meta, v_meta, k_tail, v_tail, lam_qk, subln_g, *, causal, tail_valid, tq, tk):
    bsz, lq, _ = q.shape
    lk = k.shape[1]
    assert lq % tq == 0 and lk % tk == 0 and tk % LANES == 0, (lq, tq, lk, tk)
    nq, nk = lq // tq, lk // tk
    if causal:
        pairs = [(i, j) for i in range(nq) for j in range(i + 1)]
    else:
        pairs = [(i, j) for i in range(nq) for j in range(nk)]
    qi = jnp.asarray([p[0] for p in pairs], jnp.int32)
    ki = jnp.asarray([p[1] for p in pairs], jnp.int32)
    per_batch_tail = k_tail.shape[0] > 1
    tail_map = lambda b, h, s, qi, ki: (b if per_batch_tail else 0, 0, h)
    meta_map = lambda b, h, s, qi, ki: (0, 0, h)
    grid_spec = pltpu.PrefetchScalarGridSpec(
        num_scalar_prefetch=2,
        grid=(bsz, H_A, len(pairs)),
        in_specs=[
            pl.BlockSpec((1, tq, HEAD_A), lambda b, h, s, qi, ki: (b, qi[s], h)),
            pl.BlockSpec((1, tk, HEAD_A), lambda b, h, s, qi, ki: (b, ki[s], h)),
            pl.BlockSpec((1, tk, HEAD_A), lambda b, h, s, qi, ki: (b, ki[s], h)),
            pl.BlockSpec((1, LANES, HEAD_A), meta_map),
            pl.BlockSpec((1, LANES, HEAD_A), meta_map),
            pl.BlockSpec((1, LANES, HEAD_A), tail_map),
            pl.BlockSpec((1, LANES, HEAD_A), tail_map),
            pl.BlockSpec((4, DK_A), lambda b, h, s, qi, ki: (0, 0)),
            pl.BlockSpec((1, DV_A), lambda b, h, s, qi, ki: (0, 0)),
        ],
        out_specs=pl.BlockSpec((1, tq, HEAD_A), lambda b, h, s, qi, ki: (b, qi[s], h)),
        scratch_shapes=[pltpu.VMEM((2, tq, LANES), F32), pltpu.VMEM((2, tq, LANES), F32),
                        pltpu.VMEM((2, tq, DV_A), F32)],
    )
    return pl.pallas_call(
        functools.partial(_attn_kernel, tq=tq, tk=tk, causal=causal, tail_valid=tail_valid, n_kv=nk),
        grid_spec=grid_spec,
        out_shape=jax.ShapeDtypeStruct((bsz, lq, H_A * HEAD_A), BF16),
        compiler_params=_params("parallel", "parallel", "arbitrary"),
        name="diff_attn_causal" if causal else "diff_attn_cache",
    )(qi, ki, q, k, v, k_meta, v_meta, k_tail, v_tail, lam_qk, subln_g)


def _hgrn_kernel(q_ref, k_ref, lf_ref, v_ref, gr_ref, s0_ref, g_ref, sel_ref, o_ref, sout_ref, st_ref, *, tb):
    t = pl.program_id(2)

    @pl.when(t == 0)
    def _load_state():
        st_ref[...] = s0_ref[0, 0].T

    row = lax.broadcasted_iota(jnp.int32, (CHUNK, DK_R), 0)
    sub = row // SUB
    tin = row % SUB
    same_sub = (lax.broadcasted_iota(jnp.int32, (CHUNK, CHUNK), 0) // SUB
                == lax.broadcasted_iota(jnp.int32, (CHUNK, CHUNK), 1) // SUB)
    g = g_ref[...]
    nsub = CHUNK // SUB

    def chunk(c, carry):
        sl = pl.ds(pl.multiple_of(c * CHUNK, CHUNK), CHUNK)
        qf = q_ref[0, sl, :].astype(F32)
        kf = k_ref[0, sl, :].astype(F32)
        vb = v_ref[0, sl, :]
        b = lf_ref[0, sl, :]
        sh = 1
        while sh < CHUNK:
            b = b + jnp.where(row >= sh, pltpu.roll(b, sh, 0), 0.0)
            sh *= 2
        b = b * LOG2_E
        b_last = b[CHUNK - 1:CHUNK, :]
        st = st_ref[...]
        qdec = (qf * jnp.exp2(b)).astype(BF16)
        o = lax.dot_general(qdec, st.astype(BF16), NT_DIMS, preferred_element_type=F32)
        ends = [b[SUB * j + SUB - 1:SUB * j + SUB, :] for j in range(nsub)]
        end_of_sub = jnp.concatenate([jnp.broadcast_to(e, (SUB, DK_R)) for e in ends], axis=0)
        kk = kf * jnp.exp2(end_of_sub - b)
        qs, ks = [], []
        for j in range(nsub - 1):
            e = jnp.exp2(jnp.where(sub > j, b - ends[j], -jnp.inf))
            qs.append((qf * e).astype(BF16))
            ks.append(jnp.where(sub == j, kk, 0.0).astype(BF16))
        sc = lax.dot_general(jnp.concatenate(qs, axis=1), jnp.concatenate(ks, axis=1), NT_DIMS,
                             preferred_element_type=F32)
        k3 = kf.reshape(nsub, SUB, DK_R)
        b3 = b.reshape(nsub, SUB, DK_R)
        prods = []
        for s in range(SUB):
            shp = (nsub, SUB, DK_R)
            ks_b = jnp.broadcast_to(k3[:, s:s + 1, :], shp).reshape(CHUNK, DK_R)
            bs_b = jnp.broadcast_to(b3[:, s:s + 1, :], shp).reshape(CHUNK, DK_R)
            db = b - bs_b
            a = qf * ks_b * jnp.exp2(db if s == 0 else jnp.where(tin >= s, db, -jnp.inf))
            prods.append(a.astype(BF16))
        sd = jnp.dot(jnp.concatenate(prods, axis=1), sel_ref[...], preferred_element_type=F32)
        sc = sc + jnp.where(same_sub, sd, 0.0)
        o = o + jnp.dot(sc.astype(BF16), vb, preferred_element_type=F32)
        kdec = (kf * jnp.exp2(b_last - b)).astype(BF16)
        st_ref[...] = jnp.exp2(b_last) * st + lax.dot_general(vb, kdec, TN_DIMS, preferred_element_type=F32)
        ms = jnp.mean(o * o, axis=1, keepdims=True)
        gr = gr_ref[0, sl, :]
        o_ref[0, sl, :] = (o * lax.rsqrt(ms + LN_EPS) * g * (gr * jax.nn.sigmoid(gr))).astype(o_ref.dtype)
        return carry

    lax.fori_loop(0, tb // CHUNK, chunk, 0, unroll=min(HGRN_UNROLL, tb // CHUNK))

    @pl.when(t == pl.num_programs(2) - 1)
    def _store_state():
        sout_ref[0, 0] = st_ref[...].T


def _hgrn(q, k, lf, v, g3, s0, rnorm_g):
    bsz, length, _ = q.shape
    tb = _tile(length, 2048, CHUNK)
    tok = lambda b, h, t: (b, t, h)
    sel = (lax.broadcasted_iota(jnp.int32, (SUB * DK_R, CHUNK), 0) // DK_R
           == lax.broadcasted_iota(jnp.int32, (SUB * DK_R, CHUNK), 1) % SUB).astype(BF16)
    return pl.pallas_call(
        functools.partial(_hgrn_kernel, tb=tb),
        grid=(bsz, H_R, length // tb),
        in_specs=[pl.BlockSpec((1, tb, DK_R), tok), pl.BlockSpec((1, tb, DK_R), tok),
                  pl.BlockSpec((1, tb, DK_R), tok), pl.BlockSpec((1, tb, DV_R), tok),
                  pl.BlockSpec((1, tb, DV_R), tok),
                  pl.BlockSpec((1, 1, DK_R, DV_R), lambda b, h, t: (b, h, 0, 0)),
                  pl.BlockSpec((1, DV_R), lambda b, h, t: (0, 0)),
                  pl.BlockSpec((SUB * DK_R, CHUNK), lambda b, h, t: (0, 0))],
        out_specs=[pl.BlockSpec((1, tb, DV_R), tok),
                   pl.BlockSpec((1, 1, DK_R, DV_R), lambda b, h, t: (b, h, 0, 0))],
        out_shape=[jax.ShapeDtypeStruct((bsz, length, D_MODEL), BF16),
                   jax.ShapeDtypeStruct((bsz, H_R, DK_R, DV_R), F32)],
        scratch_shapes=[pltpu.VMEM((DV_R, DK_R), F32)],
        compiler_params=_params("parallel", "parallel", "arbitrary"),
        name="hgrn2",
    )(q, k, lf, v, g3, s0, rnorm_g, sel)


def _merge_kernel(a_ref, r_ref, wa_ref, wb_ref, ga_ref, gb_ref, o_ref):
    ya = jnp.dot(a_ref[...], wa_ref[...], preferred_element_type=F32)
    yr = jnp.dot(r_ref[...], wb_ref[...], preferred_element_type=F32)
    o_ref[...] = (jax.nn.sigmoid(ga_ref[...]) * ya + jax.nn.sigmoid(gb_ref[...]) * yr).astype(o_ref.dtype)


def _merge(a_n, r_n, w_pa, w_pb, g3):
    m = a_n.shape[0]
    tm = _tile(m, 512, 16)
    tn = 1024
    nj = D_MODEL // tn
    return pl.pallas_call(
        _merge_kernel,
        grid=(m // tm, nj),
        in_specs=[pl.BlockSpec((tm, D_MODEL), lambda i, j: (i, 0)),
                  pl.BlockSpec((tm, D_MODEL), lambda i, j: (i, 0)),
                  pl.BlockSpec((D_MODEL, tn), lambda i, j: (0, j)),
                  pl.BlockSpec((D_MODEL, tn), lambda i, j: (0, j)),
                  pl.BlockSpec((tm, tn), lambda i, j: (i, j + nj)),
                  pl.BlockSpec((tm, tn), lambda i, j: (i, j + 2 * nj))],
        out_specs=pl.BlockSpec((tm, tn), lambda i, j: (i, j)),
        out_shape=jax.ShapeDtypeStruct((m, D_MODEL), BF16),
        compiler_params=_params("parallel", "parallel"),
        name="merge",
    )(a_n, r_n, w_pa, w_pb, g3, g3)


def _post_kernel(m_ref, wo_ref, x_ref, g0_ref, b0_ref, g1_ref, b1_ref, wr_ref, br_ref,
                 h32_ref, hp_ref, te_ref, tg_ref, *, band):
    for r0 in range(0, m_ref.shape[0], band):
        rows = slice(r0, r0 + band)
        mix = jnp.dot(m_ref[rows, :], wo_ref[...], preferred_element_type=F32)
        x = _layer_norm(x_ref[rows, :], g0_ref[...], b0_ref[...])
        h = _layer_norm(DEEPNORM_ALPHA * x + mix, g1_ref[...], b1_ref[...])
        h32_ref[rows, :] = h
        _store_packed(hp_ref.at[pl.ds(r0 * ROW_SUB, band * ROW_SUB)], h, band)
        logits = jnp.dot(h, wr_ref[...], preferred_element_type=F32, precision=lax.Precision.HIGHEST)
        logits = logits + br_ref[...]
        col = lax.broadcasted_iota(jnp.int32, logits.shape, 1)
        vals, idxs = [], []
        for _ in range(TOP_K):
            mx = jnp.max(logits, axis=1, keepdims=True)
            idx = jnp.min(jnp.where(logits == mx, col, LANES), axis=1, keepdims=True)
            vals.append(mx)
            idxs.append(idx)
            logits = jnp.where(col == idx, -jnp.inf, logits)
        es = [jnp.exp(v - vals[0]) for v in vals]
        den = es[0] + es[1] + es[2] + es[3]
        te = jnp.zeros(col.shape, jnp.int32)
        tg = jnp.zeros(col.shape, F32)
        for kk in range(TOP_K):
            te = jnp.where(col == kk, idxs[kk], te)
            tg = jnp.where(col == kk, es[kk] / den, tg)
        te_ref[rows, :] = te
        tg_ref[rows, :] = tg


def _post(mrg, w_o, x, ln0_g, ln0_b, ln1_g, ln1_b, w_r, b_r):
    m = mrg.shape[0]
    tm = _tile(m, 512, 16)
    band = _tile(tm, 256, 16)
    row = lambda i: (i, 0)
    fix = lambda i: (0, 0)
    once = dict(pipeline_mode=pl.Buffered(1))
    return pl.pallas_call(
        functools.partial(_post_kernel, band=band),
        grid=(m // tm,),
        in_specs=[pl.BlockSpec((tm, D_MODEL), row), pl.BlockSpec((D_MODEL, D_MODEL), fix, **once),
                  pl.BlockSpec((tm, D_MODEL), row),
                  pl.BlockSpec((1, D_MODEL), fix), pl.BlockSpec((1, D_MODEL), fix),
                  pl.BlockSpec((1, D_MODEL), fix), pl.BlockSpec((1, D_MODEL), fix),
                  pl.BlockSpec((D_MODEL, LANES), fix, **once), pl.BlockSpec((1, LANES), fix)],
        out_specs=[pl.BlockSpec((tm, D_MODEL), row), pl.BlockSpec((tm * ROW_SUB, LANES), row),
                   pl.BlockSpec((tm, LANES), row), pl.BlockSpec((tm, LANES), row)],
        out_shape=[jax.ShapeDtypeStruct((m, D_MODEL), F32), jax.ShapeDtypeStruct((m * ROW_SUB, LANES), jnp.uint32),
                   jax.ShapeDtypeStruct((m, LANES), jnp.int32), jax.ShapeDtypeStruct((m, LANES), F32)],
        compiler_params=_params("parallel"),
        name="post",
    )(mrg, w_o, x, ln0_g, ln0_b, ln1_g, ln1_b, w_r, b_r)


def _deint_kernel(w_ref, p_ref, g_ref, l_ref, *, tc):
    y = jnp.dot(w_ref[0].astype(BF16), p_ref[...], preferred_element_type=F32)
    g_ref[0] = y[:, :tc].astype(BF16)
    l_ref[0] = y[:, tc:].astype(BF16)


def _deinterleave(w1):
    tc, tr = 512, 1024
    r = lax.broadcasted_iota(jnp.int32, (2 * tc, 2 * tc), 0)
    c = lax.broadcasted_iota(jnp.int32, (2 * tc, 2 * tc), 1)
    perm = jnp.where(c < tc, r == 2 * c, r == 2 * (c - tc) + 1).astype(BF16)
    out = jax.ShapeDtypeStruct((N_EXPERTS, D_MODEL, D_FF), BF16)
    return pl.pallas_call(
        functools.partial(_deint_kernel, tc=tc),
        grid=(N_EXPERTS, D_MODEL // tr, D_FF // tc),
        in_specs=[pl.BlockSpec((1, tr, 2 * tc), lambda e, i, j: (e, i, j)),
                  pl.BlockSpec((2 * tc, 2 * tc), lambda e, i, j: (0, 0))],
        out_specs=[pl.BlockSpec((1, tr, tc), lambda e, i, j: (e, i, j)),
                   pl.BlockSpec((1, tr, tc), lambda e, i, j: (e, i, j))],
        out_shape=[out, out],
        compiler_params=_params("parallel", "parallel", "parallel"),
        name="w1_deinterleave",
    )(w1, perm)


ROW_SUB = 8
HALF = D_MODEL // 2


def _store_packed(o_ref, x, n):
    bits = lax.bitcast_convert_type(x.astype(BF16).astype(F32), jnp.uint32)
    for c in range(ROW_SUB):
        lo = bits[:, c * LANES:(c + 1) * LANES]
        hi = bits[:, HALF + c * LANES:HALF + (c + 1) * LANES]
        o_ref[pl.ds(c, n, stride=ROW_SUB), :] = (lo >> 16) | hi


def _load_packed(x_ref, c, n):
    w = x_ref[pl.ds(c, n, stride=ROW_SUB), :]
    lo = lax.bitcast_convert_type(w << 16, F32)
    hi = lax.bitcast_convert_type(w & jnp.uint32(0xFFFF0000), F32)
    return lo, hi


def _gather_kernel(idx_ref, src_ref, o_ref, sem, *, g):
    def row_copy(r, src_row):
        return pltpu.make_async_copy(src_ref.at[pl.ds(pl.multiple_of(src_row * ROW_SUB, ROW_SUB), ROW_SUB)],
                                     o_ref.at[pl.ds(pl.multiple_of(r * ROW_SUB, ROW_SUB), ROW_SUB)], sem)

    def issue(r, carry):
        row_copy(r, idx_ref[r]).start()
        return carry

    def drain(r, carry):
        row_copy(r, 0).wait()
        return carry

    lax.fori_loop(0, g, issue, 0, unroll=8)
    lax.fori_loop(0, g, drain, 0, unroll=8)


def _gather_rows(src, idx):
    n = idx.shape[0]
    g = _tile(n, 512, 128)
    return pl.pallas_call(
        functools.partial(_gather_kernel, g=g),
        grid=(n // g,),
        in_specs=[pl.BlockSpec((g,), lambda i: (i,), memory_space=pltpu.SMEM),
                  pl.BlockSpec(memory_space=pl.ANY)],
        out_specs=pl.BlockSpec((g * ROW_SUB, LANES), lambda i: (i, 0)),
        out_shape=jax.ShapeDtypeStruct((n * ROW_SUB, LANES), jnp.uint32),
        scratch_shapes=[pltpu.SemaphoreType.DMA(())],
        compiler_params=_params("arbitrary"),
        name="gather_rows",
    )(idx, src)


def _expert_kernel(be_ref, nb_ref, idx0_ref, idxn_ref, hp_ref, wg_ref, wl_ref, bg_ref, bl_ref, w2_ref, b2_ref,
                   o_ref, acc_ref, x16_ref, xbuf_ref, sem, *, nf):
    i = pl.program_id(0)
    f = pl.program_id(1)
    nblk = pl.num_programs(0)
    used = i < nb_ref[0]
    blk = x16_ref.shape[0]
    per = blk // nf
    slot = i % 2
    nxt = 1 - slot

    def row_copy(s, r, src_row):
        return pltpu.make_async_copy(hp_ref.at[pl.ds(pl.multiple_of(src_row * ROW_SUB, ROW_SUB), ROW_SUB)],
                                     xbuf_ref.at[s, pl.ds(pl.multiple_of(r * ROW_SUB, ROW_SUB), ROW_SUB)],
                                     sem.at[s])

    def drain(s):
        def body(r, carry):
            row_copy(s, r, 0).wait()
            return carry
        lax.fori_loop(0, blk, body, 0, unroll=8)

    @pl.when(jnp.logical_and(i == 0, f == 0))
    def _first_rows():
        def body(r, carry):
            row_copy(0, r, idx0_ref[r]).start()
            return carry
        lax.fori_loop(0, blk, body, 0, unroll=8)

    def prefetch_next():
        for u in range(per):
            r = f * per + u
            row_copy(nxt, r, idxn_ref[r]).start()

    @pl.when(f == 0)
    def _rows_ready():
        drain(slot)

    @pl.when(jnp.logical_and(used, f == 0))
    def _unpack():
        x_ref = xbuf_ref.at[slot]
        for c in range(ROW_SUB):
            lo, hi = _load_packed(x_ref, c, blk)
            x16_ref[:, c * LANES:(c + 1) * LANES] = lo.astype(BF16)
            x16_ref[:, HALF + c * LANES:HALF + (c + 1) * LANES] = hi.astype(BF16)
        acc_ref[...] = jnp.broadcast_to(b2_ref[0], acc_ref.shape)

    @pl.when(used)
    def _compute():
        prefetch_next()
        x = x16_ref[...]
        hg = jnp.dot(x, wg_ref[0], preferred_element_type=F32) + bg_ref[0]
        hl = jnp.dot(x, wl_ref[0], preferred_element_type=F32) + bl_ref[0]
        glu = jnp.minimum(hg, SWIGLU_LIMIT)
        lin = jnp.clip(hl, -SWIGLU_LIMIT, SWIGLU_LIMIT)
        act = (glu * jax.nn.sigmoid(SWIGLU_ALPHA * glu) * (lin + 1.0)).astype(BF16)
        acc_ref[...] += jnp.dot(act, w2_ref[0], preferred_element_type=F32)

        @pl.when(f == nf - 1)
        def _store():
            _store_packed(o_ref, acc_ref[...], blk)

    @pl.when(jnp.logical_not(used))
    def _unused():
        prefetch_next()

        @pl.when(f == nf - 1)
        def _zero():
            o_ref[...] = jnp.zeros(o_ref.shape, o_ref.dtype)

    @pl.when(jnp.logical_and(i == nblk - 1, f == nf - 1))
    def _last_rows():
        drain(nxt)


def _experts(hp, row_tok, blk_e, n_used, w1g, w1l, b1g, b1l, w2, b2, blk):
    rows = row_tok.shape[0]
    nblk = rows // blk
    tf = 1024
    nf = D_FF // tf
    grid_spec = pltpu.PrefetchScalarGridSpec(
        num_scalar_prefetch=2,
        grid=(nblk, nf),
        in_specs=[
            pl.BlockSpec((blk,), lambda i, f, be, nb: (0,), memory_space=pltpu.SMEM),
            pl.BlockSpec((blk,), lambda i, f, be, nb: (jnp.minimum(i + 1, nblk - 1),), memory_space=pltpu.SMEM),
            pl.BlockSpec(memory_space=pl.ANY),
            pl.BlockSpec((1, D_MODEL, tf), lambda i, f, be, nb: (be[i], 0, f)),
            pl.BlockSpec((1, D_MODEL, tf), lambda i, f, be, nb: (be[i], 0, f)),
            pl.BlockSpec((1, 1, tf), lambda i, f, be, nb: (be[i], 0, f)),
            pl.BlockSpec((1, 1, tf), lambda i, f, be, nb: (be[i], 0, f)),
            pl.BlockSpec((1, tf, D_MODEL), lambda i, f, be, nb: (be[i], f, 0)),
            pl.BlockSpec((1, 1, D_MODEL), lambda i, f, be, nb: (be[i], 0, 0)),
        ],
        out_specs=pl.BlockSpec((blk * ROW_SUB, LANES), lambda i, f, be, nb: (i, 0)),
        scratch_shapes=[pltpu.VMEM((blk, D_MODEL), F32), pltpu.VMEM((blk, D_MODEL), BF16),
                        pltpu.VMEM((2, blk * ROW_SUB, LANES), jnp.uint32), pltpu.SemaphoreType.DMA((2,))],
    )
    return pl.pallas_call(
        functools.partial(_expert_kernel, nf=nf),
        grid_spec=grid_spec,
        out_shape=jax.ShapeDtypeStruct((rows * ROW_SUB, LANES), jnp.uint32),
        compiler_params=_params("arbitrary", "arbitrary"),
        name="experts",
    )(blk_e, n_used, row_tok, row_tok, hp, w1g, w1l, b1g, b1l, w2, b2)


def _combine_kernel(h_ref, y_ref, tg_ref, g_ref, b_ref, o_ref):
    tm = h_ref.shape[0]
    tg = tg_ref[...]
    los, his = [], []
    for c in range(ROW_SUB):
        lo = hi = None
        for kk in range(TOP_K):
            ylo, yhi = _load_packed(y_ref.at[kk], c, tm)
            gate = tg[:, kk:kk + 1]
            lo = gate * ylo if lo is None else lo + gate * ylo
            hi = gate * yhi if hi is None else hi + gate * yhi
        los.append(lo)
        his.append(hi)
    f = jnp.concatenate(los + his, axis=1)
    o_ref[...] = _layer_norm(DEEPNORM_ALPHA * h_ref[...] + f, g_ref[...], b_ref[...])


def _combine(h32, yg, tg, ln2_g, ln2_b, row_off):
    m = h32.shape[0]
    tm = _tile(m, 256, 16)
    off = row_off // tm
    return pl.pallas_call(
        _combine_kernel,
        grid=(m // tm,),
        in_specs=[pl.BlockSpec((tm, D_MODEL), lambda i: (i, 0)),
                  pl.BlockSpec((TOP_K, tm * ROW_SUB, LANES), lambda i: (0, i + off, 0)),
                  pl.BlockSpec((tm, LANES), lambda i: (i, 0)),
                  pl.BlockSpec((1, D_MODEL), lambda i: (0, 0)),
                  pl.BlockSpec((1, D_MODEL), lambda i: (0, 0))],
        out_specs=pl.BlockSpec((tm, D_MODEL), lambda i: (i, 0)),
        out_shape=jax.ShapeDtypeStruct((m, D_MODEL), F32),
        compiler_params=_params("parallel"),
        name="combine",
    )(h32, yg, tg, ln2_g, ln2_b)


def _route(top_e, blk):
    t = top_e.shape[0]
    n = t * TOP_K
    flat_e = top_e.reshape(-1)
    onehot = (flat_e[:, None] == jnp.arange(N_EXPERTS, dtype=jnp.int32)[None, :]).astype(jnp.int32)
    csum = jnp.cumsum(onehot, axis=0)
    rank = jnp.sum(onehot * csum, axis=1) - 1
    counts = csum[-1]
    padded = (counts + blk - 1) // blk * blk
    pad_end = jnp.cumsum(padded)
    pad_start = pad_end - padded
    start = jnp.cumsum(counts) - counts
    dest = (pad_start[flat_e] + rank).astype(jnp.int32)
    n_blocks = -(-n // blk) + N_EXPERTS
    blk_e = jnp.minimum(jnp.searchsorted(pad_end, jnp.arange(n_blocks, dtype=jnp.int32) * blk, side="right"),
                        N_EXPERTS - 1).astype(jnp.int32)
    order = jnp.argsort(flat_e, stable=True).astype(jnp.int32)
    r = jnp.arange(n_blocks * blk, dtype=jnp.int32)
    row_e = jnp.repeat(blk_e, blk)
    j = r - pad_start[row_e]
    valid = j < counts[row_e]
    row_tok = jnp.where(valid, order[jnp.clip(start[row_e] + j, 0, n - 1)] // TOP_K, 0).astype(jnp.int32)
    n_used = (pad_end[-1:] // blk).astype(jnp.int32)
    return row_tok, blk_e, n_used, dest.reshape(t, TOP_K)


def _pad_rows(a, n):
    return jnp.pad(a, ((0, n - a.shape[0]),) + ((0, 0),) * (a.ndim - 1))


def _forward(x_prompt, x_sample, cache_k, cache_v, state_hgrn, meta_tokens, ln0_g, ln0_b, w_in, lam_qk,
             subln_g, lb_logits, rnorm_g, w_pa, w_pb, w_o, ln1_g, ln1_b, ln2_g, ln2_b,
             w_router, b_router, w1, b1, w2, b2, *, expert_block, attn_tile):
    bsz, seq, _ = x_prompt.shape
    dec_b, dec_t, _ = x_sample.shape
    past = cache_k.shape[2]
    n_s = dec_b * dec_t
    n_p = bsz * seq

    row = lambda a: a.reshape(1, -1).astype(F32)
    g0, b0, g1, b1n, g2, b2n = row(ln0_g), row(ln0_b), row(ln1_g[0]), row(ln1_b[0]), row(ln2_g[0]), row(ln2_b[0])
    lb = jnp.cumsum(jax.nn.softmax(lb_logits.astype(F32), axis=0), axis=0)[0].reshape(1, -1)
    w_in16 = w_in[0].astype(BF16)
    w_pa16, w_pb16, w_o16 = w_pa[0].astype(BF16), w_pb[0].astype(BF16), w_o[0].astype(BF16)
    w_r = jnp.pad(w_router[0].astype(F32), ((0, 0), (0, LANES - N_EXPERTS)))
    b_r = jnp.pad(b_router[0].astype(F32), (0, LANES - N_EXPERTS), constant_values=-jnp.inf).reshape(1, LANES)
    w1g, w1l = _deinterleave(w1[0])
    b1g = b1[0, :, 0::2].reshape(N_EXPERTS, 1, D_FF).astype(F32)
    b1l = b1[0, :, 1::2].reshape(N_EXPERTS, 1, D_FF).astype(F32)
    w2_16 = w2[0].astype(BF16)
    b2r = b2[0].reshape(N_EXPERTS, 1, D_MODEL).astype(F32)
    lam = lam_qk[0].astype(F32)
    sub_g = subln_g[0].reshape(1, DV_A).astype(F32)
    rn_g = rnorm_g[0].reshape(1, DV_R).astype(F32)

    xs = jnp.concatenate([x_sample.reshape(n_s, D_MODEL), meta_tokens], axis=0)
    n_sm = n_s + N_META
    pos_s = jnp.concatenate([jnp.tile(N_META + past + jnp.arange(dec_t, dtype=jnp.int32), dec_b),
                             jnp.arange(N_META, dtype=jnp.int32)])
    cos_s, sin_s = _rope_tables(pos_s)
    hs16 = _ln_bf16(xs, g0, b0)
    q_s, k32_s, k16_s, v32_s, v16_s, qr_s, kr_s, lf_s, ir_s, g3_s = _in_proj(hs16, w_in16, lb, cos_s, sin_s, n_sm)
    k_meta32, v_meta32 = k32_s[n_s:], v32_s[n_s:]
    k_meta16 = _pad_rows(k16_s[n_s:], LANES)[None]
    v_meta16 = _pad_rows(v16_s[n_s:], LANES)[None]

    tail_pad = ((0, 0), (0, LANES - dec_t), (0, 0))
    k_new = jnp.pad(k16_s[:n_s].reshape(dec_b, dec_t, D_MODEL), tail_pad)
    v_new = jnp.pad(v16_s[:n_s].reshape(dec_b, dec_t, D_MODEL), tail_pad)
    a_s = _diff_attn(q_s[:n_s].reshape(dec_b, dec_t, D_MODEL),
                     cache_k[0].reshape(dec_b, past, D_MODEL), cache_v[0].reshape(dec_b, past, D_MODEL),
                     k_meta16, v_meta16, k_new, v_new, lam, sub_g,
                     causal=False, tail_valid=dec_t, tq=dec_t, tk=_tile(past, 1024, LANES))

    n_str = dec_b + 1
    chunk_pad = ((0, 0), (0, CHUNK - dec_t), (0, 0))
    to_chunk = lambda a: jnp.pad(a.reshape(n_str, dec_t, a.shape[-1]), chunk_pad)
    s0 = jnp.concatenate([state_hgrn[0].astype(F32), jnp.zeros((1, H_R, DK_R, DV_R), F32)], axis=0)
    r_s, s_out = _hgrn(to_chunk(qr_s), to_chunk(kr_s), to_chunk(lf_s), to_chunk(ir_s), to_chunk(g3_s), s0, rn_g)
    s_sample, s_meta = s_out[:dec_b], s_out[dec_b:]

    mrg_s = _merge(a_s.reshape(n_s, D_MODEL), r_s[:dec_b, :dec_t].reshape(n_s, D_MODEL), w_pa16, w_pb16, g3_s[:n_s])
    h32_s, hp_s, te_s, tg_s = _post(mrg_s, w_o16, x_sample.reshape(n_s, D_MODEL), g0, b0, g1, b1n, w_r, b_r)

    xp = x_prompt.reshape(n_p, D_MODEL)
    cos_p, sin_p = _rope_tables(N_META + jnp.arange(seq, dtype=jnp.int32))
    hp16 = _ln_bf16(xp, g0, b0)
    q_p, k32_p, k16_p, v32_p, v16_p, qr_p, kr_p, lf_p, ir_p, g3_p = _in_proj(hp16, w_in16, lb, cos_p, sin_p, seq)
    t3 = lambda a: a.reshape(bsz, seq, a.shape[-1])
    ta = _tile(seq, attn_tile, CHUNK)
    a_p = _diff_attn(t3(q_p), t3(k16_p), t3(v16_p), k_meta16, v_meta16, k_meta16, v_meta16, lam, sub_g,
                     causal=True, tail_valid=0, tq=ta, tk=ta)
    r_p, s_prompt = _hgrn(t3(qr_p), t3(kr_p), t3(lf_p), t3(ir_p), t3(g3_p),
                          jnp.broadcast_to(s_meta, (bsz, H_R, DK_R, DV_R)), rn_g)
    mrg_p = _merge(a_p.reshape(n_p, D_MODEL), r_p.reshape(n_p, D_MODEL), w_pa16, w_pb16, g3_p)
    h32_p, hp_p, te_p, tg_p = _post(mrg_p, w_o16, xp, g0, b0, g1, b1n, w_r, b_r)

    hp = jnp.concatenate([hp_p, hp_s], axis=0)
    top_e = jnp.concatenate([te_p[:, :TOP_K], te_s[:, :TOP_K]], axis=0)
    row_tok, blk_e, n_used, dest = _route(top_e, expert_block)
    yb = _experts(hp, row_tok, blk_e, n_used, w1g, w1l, b1g, b1l, w2_16, b2r, expert_block)
    yg = _gather_rows(yb, dest.T.reshape(-1)).reshape(TOP_K, (n_p + n_s) * ROW_SUB, LANES)
    y_p = _combine(h32_p, yg, tg_p, g2, b2n, 0)
    y_s = _combine(h32_s, yg, tg_s, g2, b2n, n_p)

    k_prompt = jnp.concatenate([jnp.broadcast_to(k_meta32[None], (bsz, N_META, D_MODEL)), t3(k32_p)], axis=1)
    v_prompt = jnp.concatenate([jnp.broadcast_to(v_meta32[None], (bsz, N_META, D_MODEL)), t3(v32_p)], axis=1)
    return (y_p.reshape(bsz, seq, D_MODEL),
            y_s.reshape(dec_b, dec_t, D_MODEL),
            k_prompt.reshape(1, bsz, N_META + seq, H_A, 2, DK_A),
            v_prompt.reshape(1, bsz, N_META + seq, H_A, DV_A),
            s_prompt[None],
            k32_s[:n_s].reshape(1, dec_b, dec_t, H_A, 2, DK_A),
            v32_s[:n_s].reshape(1, dec_b, dec_t, H_A, DV_A),
            s_sample[None])


def kernel(x_prompt, x_sample, cache_k, cache_v, state_hgrn, meta_tokens, ln0_g, ln0_b, w_in, lam_qk, subln_g, lb_logits, rnorm_g, w_pa, w_pb, w_o, ln1_g, ln1_b, ln2_g, ln2_b, w_router, b_router, w1, b1, w2, b2):
    return _forward(x_prompt, x_sample, cache_k, cache_v, state_hgrn, meta_tokens, ln0_g, ln0_b, w_in, lam_qk,
                    subln_g, lb_logits, rnorm_g, w_pa, w_pb, w_o, ln1_g, ln1_b, ln2_g, ln2_b,
                    w_router, b_router, w1, b1, w2, b2, expert_block=512, attn_tile=1024)
```
